```python
import math
import jax, jax.numpy as jnp
from jax import lax
import numpy as np

D_MODEL = 1024
BATCH = 32
SEQ = 2048
DEPTH = 1
DEC_BATCH = 128
DEC_SEQ = 4
PAST_LEN = 8192
PAGE_SIZE = 128

HEAD_DIM = 128
SWA_GROUPS = ((128, 1), (512, 4), (2048, 16))
N_SWA = 3
SWA_HEADS = 4
SWA_BAND = 128
GLA_HEADS = 4
GLA_DK = 64
GLA_DV = 128
GLA_RANK = 16
GLA_TAU = 16.0
GLA_CHUNK = 64
MEM_LEN = 256
MEM_HEADS = 4
BRANCH_W = 512
N_BRANCH = 3
D_FF = 2816
ROPE_THETA = 10000.0
EPS = 1e-6
NEG = -1e30

SPLIT_SIZES = (3 * N_SWA * SWA_HEADS * HEAD_DIM,
               GLA_HEADS * GLA_DK,
               GLA_HEADS * GLA_DK,
               GLA_HEADS * GLA_DV,
               GLA_HEADS * GLA_DV,
               GLA_RANK,
               MEM_HEADS * HEAD_DIM)
SPLIT_POINTS = tuple(sum(SPLIT_SIZES[:i + 1]) for i in range(len(SPLIT_SIZES) - 1))
D_IN = sum(SPLIT_SIZES)

kernel_name = 'hybrid_dilated_gla_memory_decode_step'


def _rmsnorm(x, g):
    xf = x.astype(jnp.float32)
    y = xf * lax.rsqrt(jnp.mean(xf * xf, axis=-1, keepdims=True) + EPS)
    return (y * g.astype(jnp.float32)).astype(x.dtype)


def _swiglu(x, w_in, w_out):
    a, b = jnp.split(x @ w_in, 2, axis=-1)
    return (jax.nn.silu(a) * b) @ w_out


def _rope(x, pos):
    half = HEAD_DIM // 2
    inv = ROPE_THETA ** (-jnp.arange(half, dtype=jnp.float32) / half)
    ang = pos.astype(jnp.float32)[:, None] * inv[None, :]
    shape = (1, ang.shape[0]) + (1,) * (x.ndim - 3) + (half,)
    cos = jnp.cos(ang).reshape(shape)
    sin = jnp.sin(ang).reshape(shape)
    xf = x.astype(jnp.float32)
    x1, x2 = xf[..., :half], xf[..., half:]
    return jnp.concatenate([x1 * cos - x2 * sin, x2 * cos + x1 * sin], axis=-1).astype(x.dtype)


def _dilated_prompt(q, k, v, dil):
    B, S, H, E = q.shape
    M = S // dil
    nb = -(-M // SWA_BAND)
    Mp = nb * SWA_BAND

    def strided(t, front):
        t = t.astype(jnp.float32).reshape(B, M, dil, H, E)
        return jnp.pad(t, ((0, 0), (front, Mp - M), (0, 0), (0, 0), (0, 0)))

    def band(t):
        t = strided(t, SWA_BAND).reshape(B, nb + 1, SWA_BAND, dil, H, E)
        return jnp.concatenate([t[:, :-1], t[:, 1:]], axis=2)

    qb = strided(q, 0).reshape(B, nb, SWA_BAND, dil, H, E)
    kb, vb = band(k), band(v)
    s = jnp.einsum('bnqrhe,bnkrhe->bnrhqk', qb, kb) * (E ** -0.5)
    qi = jnp.arange(SWA_BAND)[:, None]
    kj = jnp.arange(2 * SWA_BAND)[None, :]
    dist = SWA_BAND + qi - kj
    key_m = jnp.arange(nb)[:, None, None] * SWA_BAND + kj[None] - SWA_BAND
    valid = (dist >= 0)[None] & (dist <= SWA_BAND)[None] & (key_m >= 0)
    s = jnp.where(valid[None, :, None, None], s, NEG)
    lse = jax.nn.logsumexp(s, axis=-1)
    p = jnp.exp(s - lse[..., None])
    o = jnp.einsum('bnrhqk,bnkrhe->bnqrhe', p, vb)
    o = o.reshape(B, Mp, dil, H, E)[:, :M].reshape(B, S, H, E)
    lse = lse.transpose(0, 1, 4, 2, 3).reshape(B, Mp, dil, H)[:, :M].reshape(B, S, H)
    return o, lse


def _dilated_sample(q, k, v, buf, dil):
    L = buf.shape[1]
    T = q.shape[1]
    k_all = jnp.concatenate([buf[:, :, 0].astype(jnp.float32), k.astype(jnp.float32)], axis=1)
    v_all = jnp.concatenate([buf[:, :, 1].astype(jnp.float32), v.astype(jnp.float32)], axis=1)
    idx = L + jnp.arange(T)[:, None] - dil * jnp.arange(SWA_BAND + 1)[None, :]
    valid = idx >= 0
    idx = jnp.maximum(idx, 0)
    kg = k_all[:, idx]
    vg = v_all[:, idx]
    s = jnp.einsum('bthe,btnhe->bthn', q.astype(jnp.float32), kg) * (q.shape[-1] ** -0.5)
    s = jnp.where(valid[None, :, None, :], s, NEG)
    lse = jax.nn.logsumexp(s, axis=-1)
    p = jnp.exp(s - lse[..., None])
    o = jnp.einsum('bthn,btnhe->bthe', p, vg)
    return o, lse


def _gla(q, k, v, log_a, s0):
    B, T, H, DK = q.shape
    C = GLA_CHUNK if T % GLA_CHUNK == 0 else T
    nc = T // C

    def chunks(t):
        return t.astype(jnp.float32).reshape((B, nc, C) + t.shape[2:]).swapaxes(0, 1)

    causal = jnp.tril(jnp.ones((C, C), dtype=bool))

    def step(s, inp):
        qc, kc, vc, lc = inp
        b = jnp.cumsum(lc, axis=1)
        b_end = b[:, -1]
        qe = qc * jnp.exp(b)
        ke = kc * jnp.exp(-b)
        att = jnp.where(causal, jnp.einsum('bihk,bjhk->bhij', qe, ke), 0.0)
        o = jnp.einsum('bchk,bhkv->bchv', qe, s) + jnp.einsum('bhij,bjhv->bihv', att, vc)
        kd = kc * jnp.exp(b_end[:, None] - b)
        s = s * jnp.exp(b_end)[..., None] + jnp.einsum('bchk,bchv->bhkv', kd, vc)
        return s, o

    s, o = lax.scan(step, s0.astype(jnp.float32), (chunks(q), chunks(k), chunks(v), chunks(log_a)))
    return o.swapaxes(0, 1).reshape(B, T, H, v.shape[-1]), s


def _mem_kv(mem, g_mem, w_mem_kv):
    B = mem.shape[0]
    return (_rmsnorm(mem, g_mem) @ w_mem_kv).reshape(B, MEM_LEN, 2, MEM_HEADS, HEAD_DIM)


def _mem_attend(q, kv):
    s = jnp.einsum('bthe,bmhe->bhtm', q.astype(jnp.float32), kv[:, :, 0].astype(jnp.float32)) * (HEAD_DIM ** -0.5)
    p = jax.nn.softmax(s, axis=-1)
    return jnp.einsum('bhtm,bmhe->bthe', p, kv[:, :, 1].astype(jnp.float32))


def _layer(x, pos, mem_kv, gla_s0, swa_bufs, g_ffn1, w_ffn1_in, w_ffn1_out, g_mix, w_in,
           w_gla_a2, b_gla_a, g_gla_out, w_gate, w_branch, w_out, g_ffn2, w_ffn2_in, w_ffn2_out):
    B, T, _ = x.shape
    h = x + 0.5 * _swiglu(_rmsnorm(x, g_ffn1), w_ffn1_in, w_ffn1_out)
    u = _rmsnorm(h, g_mix)
    z_swa, z_gq, z_gk, z_gv, z_gr, z_ga, z_mq = jnp.split(u @ w_in, SPLIT_POINTS, axis=-1)

    qkv = z_swa.reshape(B, T, 3, N_SWA, SWA_HEADS, HEAD_DIM)
    q = _rope(qkv[:, :, 0], pos)
    k = _rope(qkv[:, :, 1], pos)
    v = qkv[:, :, 2]
    outs, lses, rows = [], [], []
    for g, (win, dil) in enumerate(SWA_GROUPS):
        if swa_bufs is None:
            o_g, lse_g = _dilated_prompt(q[:, :, g], k[:, :, g], v[:, :, g], dil)
            keep = min(win, T)
            rows.append(jnp.stack([k[:, T - keep:, g], v[:, T - keep:, g]], axis=2))
        else:
            o_g, lse_g = _dilated_sample(q[:, :, g], k[:, :, g], v[:, :, g], swa_bufs[g], dil)
            rows.append(jnp.stack([k[:, :, g], v[:, :, g]], axis=2))
        outs.append(o_g)
        lses.append(lse_g)
    wgt = jax.nn.softmax(jnp.stack(lses, axis=0), axis=0)
    o_swa = jnp.sum(wgt[..., None] * jnp.stack(outs, axis=0), axis=0).reshape(B, T, BRANCH_W).astype(x.dtype)

    gq = z_gq.reshape(B, T, GLA_HEADS, GLA_DK) * (GLA_DK ** -0.5)
    gk = z_gk.reshape(B, T, GLA_HEADS, GLA_DK)
    gv = z_gv.reshape(B, T, GLA_HEADS, GLA_DV)
    gr = z_gr.reshape(B, T, GLA_HEADS, GLA_DV)
    log_a = jax.nn.log_sigmoid((z_ga @ w_gla_a2 + b_gla_a).astype(jnp.float32)) / GLA_TAU
    o_g, gla_state = _gla(gq, gk, gv, log_a.reshape(B, T, GLA_HEADS, GLA_DK), gla_s0)
    o_gla = (_rmsnorm(o_g, g_gla_out) * jax.nn.silu(gr.astype(jnp.float32))).reshape(B, T, BRANCH_W).astype(x.dtype)

    mq = z_mq.reshape(B, T, MEM_HEADS, HEAD_DIM)
    o_mem = _mem_attend(mq, mem_kv).reshape(B, T, BRANCH_W).astype(x.dtype)

    gates = jax.nn.sigmoid(u @ w_gate).reshape(B, T, N_BRANCH, D_MODEL)
    br = jnp.einsum('btnc,ncd->btnd', jnp.stack([o_swa, o_gla, o_mem], axis=2), w_branch)
    h = h + jnp.sum(gates * br, axis=2) @ w_out

    h = h + 0.5 * _swiglu(_rmsnorm(h, g_ffn2), w_ffn2_in, w_ffn2_out)
    return h, rows, gla_state


def setup_inputs(seed: int = 0) -> dict:
    key = jax.random.key(seed)
    ks = jax.random.split(key, 32)
    f32 = jnp.float32

    def nrm(k, shape, scale):
        return jax.random.normal(k, shape, f32) * scale

    def gain(k, n):
        return 1.0 + 0.02 * jax.random.normal(k, (DEPTH, n), f32)

    lens = [min(w, PAST_LEN) for w, _ in SWA_GROUPS]
    return {
        'x_prompt': nrm(ks[0], (BATCH, SEQ, D_MODEL), 1.0),
        'x_sample': nrm(ks[1], (DEC_BATCH, DEC_SEQ, D_MODEL), 1.0),
        'mem_prompt': nrm(ks[2], (BATCH, MEM_LEN, D_MODEL), 1.0),
        'cache_swa0': nrm(ks[3], (DEPTH, DEC_BATCH, lens[0], 2, SWA_HEADS, HEAD_DIM), 1.0),
        'cache_swa1': nrm(ks[4], (DEPTH, DEC_BATCH, lens[1], 2, SWA_HEADS, HEAD_DIM), 1.0),
        'cache_swa2': nrm(ks[5], (DEPTH, DEC_BATCH, lens[2], 2, SWA_HEADS, HEAD_DIM), 1.0),
        'cache_mem_kv': nrm(ks[6], (DEPTH, DEC_BATCH, MEM_LEN, 2, MEM_HEADS, HEAD_DIM), 1.0),
        'state_gla': nrm(ks[7], (DEPTH, DEC_BATCH, GLA_HEADS, GLA_DK, GLA_DV), 0.5),
        'g_ffn1': gain(ks[8], D_MODEL),
        'w_ffn1_in': nrm(ks[9], (DEPTH, D_MODEL, 2 * D_FF), D_MODEL ** -0.5),
        'w_ffn1_out': nrm(ks[10], (DEPTH, D_FF, D_MODEL), D_FF ** -0.5),
        'g_mix': gain(ks[11], D_MODEL),
        'w_in': nrm(ks[12], (DEPTH, D_MODEL, D_IN), D_MODEL ** -0.5),
        'w_gla_a2': nrm(ks[13], (DEPTH, GLA_RANK, GLA_HEADS * GLA_DK), GLA_RANK ** -0.5),
        'b_gla_a': nrm(ks[14], (DEPTH, GLA_HEADS * GLA_DK), 0.1),
        'g_gla_out': gain(ks[15], GLA_DV),
        'w_gate': nrm(ks[16], (DEPTH, D_MODEL, N_BRANCH * D_MODEL), D_MODEL ** -0.5),
        'w_branch': nrm(ks[17], (DEPTH, N_BRANCH, BRANCH_W, D_MODEL), BRANCH_W ** -0.5),
        'w_out': nrm(ks[18], (DEPTH, D_MODEL, D_MODEL), D_MODEL ** -0.5),
        'g_mem': gain(ks[19], D_MODEL),
        'w_mem_kv': nrm(ks[20], (DEPTH, D_MODEL, 2 * MEM_HEADS * HEAD_DIM), D_MODEL ** -0.5),
        'g_ffn2': gain(ks[21], D_MODEL),
        'w_ffn2_in': nrm(ks[22], (DEPTH, D_MODEL, 2 * D_FF), D_MODEL ** -0.5),
        'w_ffn2_out': nrm(ks[23], (DEPTH, D_FF, D_MODEL), D_FF ** -0.5),
        'g_final': 1.0 + 0.02 * jax.random.normal(ks[24], (D_MODEL,), f32),
    }


def reference(x_prompt, x_sample, mem_prompt, cache_swa0, cache_swa1, cache_swa2, cache_mem_kv, state_gla,
              g_ffn1, w_ffn1_in, w_ffn1_out, g_mix, w_in, w_gla_a2, b_gla_a, g_gla_out, w_gate, w_branch,
              w_out, g_mem, w_mem_kv, g_ffn2, w_ffn2_in, w_ffn2_out, g_final):
    pos_p = jnp.arange(x_prompt.shape[1], dtype=jnp.int32)
    pos_s = PAST_LEN + jnp.arange(x_sample.shape[1], dtype=jnp.int32)
    hp, hs = x_prompt, x_sample
    swa_p = [[] for _ in range(N_SWA)]
    swa_s = [[] for _ in range(N_SWA)]
    memkv_p, gla_p, gla_s = [], [], []
    for l in range(DEPTH):
        lw = (g_ffn1[l], w_ffn1_in[l], w_ffn1_out[l], g_mix[l], w_in[l], w_gla_a2[l], b_gla_a[l],
              g_gla_out[l], w_gate[l], w_branch[l], w_out[l], g_ffn2[l], w_ffn2_in[l], w_ffn2_out[l])
        kv_p = _mem_kv(mem_prompt, g_mem[l], w_mem_kv[l])
        s0 = jnp.zeros((x_prompt.shape[0], GLA_HEADS, GLA_DK, GLA_DV), jnp.float32)
        hp, rows_p, st_p = _layer(hp, pos_p, kv_p, s0, None, *lw)
        hs, rows_s, st_s = _layer(hs, pos_s, cache_mem_kv[l], state_gla[l],
                                  (cache_swa0[l], cache_swa1[l], cache_swa2[l]), *lw)
        for g in range(N_SWA):
            swa_p[g].append(rows_p[g])
            swa_s[g].append(rows_s[g])
        memkv_p.append(kv_p)
        gla_p.append(st_p)
        gla_s.append(st_s)
    y_prompt = _rmsnorm(hp, g_final)
    y_sample = _rmsnorm(hs, g_final)
    return (y_prompt, y_sample,
            jnp.stack(swa_p[0]), jnp.stack(swa_p[1]), jnp.stack(swa_p[2]), jnp.stack(memkv_p), jnp.stack(gla_p),
            jnp.stack(swa_s[0]), jnp.stack(swa_s[1]), jnp.stack(swa_s[2]), jnp.stack(gla_s))
```

```python
import functools
import math

import numpy as np
import jax
import jax.numpy as jnp
from jax import lax
from jax.experimental import pallas as pl
from jax.experimental.pallas import tpu as pltpu

F32 = jnp.float32
BF16 = jnp.bfloat16

D_MODEL = 1024
PAST_LEN = 8192
HEAD_DIM = 128
SWA_GROUPS = ((128, 1), (512, 4), (2048, 16))
N_SWA = 3
SWA_HEADS = 4
SWA_BAND = 128
GLA_HEADS = 4
GLA_DK = 64
GLA_DV = 128
GLA_RANK = 16
GLA_TAU = 16.0
GLA_CHUNK = 64
MEM_LEN = 256
MEM_HEADS = 4
BRANCH_W = 512
N_BRANCH = 3
D_FF = 2816
ROPE_THETA = 10000.0
EPS = 1e-6
NEG = -1e30

SWA_W = N_SWA * SWA_HEADS * HEAD_DIM
GLA_QW = GLA_HEADS * GLA_DK
GLA_VW = GLA_HEADS * GLA_DV

LANES = 128
SUBLANES = 8
MXU_DIM = 256
VMEM_BYTES_V7X = 64 * 1024 * 1024
VMEM_INTERNAL_RESERVE = 10 * 1024 * 1024

TOKEN_TILE = 512
FF_CHUNK = MXU_DIM

_NT = (((1,), (1,)), ((), ()))
_TN = (((0,), (0,)), ((), ()))


def _nbytes(shape, dtype):
    return math.prod(shape) * jnp.dtype(dtype).itemsize


def _params(semantics, pipelined, resident=(), scratch=()):
    need = 2 * sum(_nbytes(s, d) for s, d in pipelined)
    need += sum(_nbytes(s, d) for s, d in resident)
    need += sum(_nbytes(s, d) for s, d in scratch)
    limit = min(need + VMEM_INTERNAL_RESERVE, VMEM_BYTES_V7X - 4 * 1024 * 1024)
    return pltpu.CompilerParams(dimension_semantics=semantics, vmem_limit_bytes=limit)


def _resident(shape):
    del shape
    return pl.BlockSpec(memory_space=pltpu.VMEM)


def _rms(x, g):
    return x * lax.rsqrt(jnp.mean(x * x, axis=-1, keepdims=True) + EPS) * g


def _dot(a, b):
    return jnp.dot(a, b, preferred_element_type=F32)


def _ffn_body(*refs, final):
    if final:
        x_ref, g_ref, wa_ref, wb_ref, wo_ref, gf_ref, o_ref, hid_ref = refs
    else:
        x_ref, g_ref, wa_ref, wb_ref, wo_ref, o_ref, hid_ref = refs
    x = x_ref[...]
    xb = _rms(x, g_ref[...]).astype(BF16)
    for c in range(D_FF // FF_CHUNK):
        sl = slice(c * FF_CHUNK, (c + 1) * FF_CHUNK)
        a = _dot(xb, wa_ref[:, sl])
        b = _dot(xb, wb_ref[:, sl])
        hid_ref[:, sl] = (a * jax.nn.sigmoid(a) * b).astype(BF16)
    y = x + 0.5 * _dot(hid_ref[...], wo_ref[...])
    if final:
        y = _rms(y, gf_ref[...])
    o_ref[...] = y


def _ffn(x, g, wa, wb, wo, g_final=None):
    n = x.shape[0]
    tm = min(TOKEN_TILE, n)
    final = g_final is not None
    tile = pl.BlockSpec((tm, D_MODEL), lambda i: (i, 0))
    args = [x, g, wa, wb, wo] + ([g_final] if final else [])
    in_specs = [tile] + [_resident(a.shape) for a in args[1:]]
    return pl.pallas_call(
        functools.partial(_ffn_body, final=final),
        grid=(n // tm,),
        in_specs=in_specs,
        out_specs=tile,
        out_shape=jax.ShapeDtypeStruct((n, D_MODEL), F32),
        scratch_shapes=[pltpu.VMEM((tm, D_FF), BF16)],
        compiler_params=_params(
            ("parallel",),
            pipelined=[((tm, D_MODEL), F32)] * 2,
            resident=[(a.shape, a.dtype) for a in args[1:]],
            scratch=[((tm, D_FF), BF16)]),
        name="ffn_final" if final else "ffn",
    )(*args)


def _proj_swa_body(h_ref, g_ref, cos_ref, sin_ref, w_ref, o_ref):
    u = _rms(h_ref[...], g_ref[...]).astype(BF16)
    cos = cos_ref[...]
    sin = sin_ref[...]
    n_slab = BRANCH_W // HEAD_DIM
    for c in range(3 * N_SWA):
        z = _dot(u, w_ref[:, c * BRANCH_W:(c + 1) * BRANCH_W])
        for j in range(n_slab):
            s = z[:, j * HEAD_DIM:(j + 1) * HEAD_DIM]
            if c < 2 * N_SWA:
                s = s * cos + pltpu.roll(s, HEAD_DIM // 2, axis=1) * sin
            if c < N_SWA:
                s = s * (HEAD_DIM ** -0.5)
            o_ref[c * n_slab + j] = s


def _proj_swa(h, g, cos, sin, w, pos_blocks):
    n = h.shape[0]
    tm = min(TOKEN_TILE, n)
    n_slabs = 3 * N_SWA * SWA_HEADS
    tile = pl.BlockSpec((tm, D_MODEL), lambda i: (i, 0))
    pos = pl.BlockSpec((tm, HEAD_DIM), lambda i: (i % pos_blocks, 0))
    return pl.pallas_call(
        _proj_swa_body,
        grid=(n // tm,),
        in_specs=[tile, _resident(g.shape), pos, pos, _resident(w.shape)],
        out_specs=pl.BlockSpec((n_slabs, tm, HEAD_DIM), lambda i: (0, i, 0)),
        out_shape=jax.ShapeDtypeStruct((n_slabs, n, HEAD_DIM), F32),
        compiler_params=_params(
            ("parallel",),
            pipelined=[((tm, D_MODEL), F32), ((tm, HEAD_DIM), F32), ((tm, HEAD_DIM), F32),
                       ((n_slabs, tm, HEAD_DIM), F32)],
            resident=[(g.shape, g.dtype), (w.shape, w.dtype)]),
        name="proj_swa",
    )(h, g, cos, sin, w)


def _log_sigmoid(x):
    return jnp.minimum(x, 0.0) - jnp.log1p(jnp.exp(-jnp.abs(x)))


def _proj_misc_body(h_ref, g_ref, wg_ref, wga_ref, wa2_ref, ba_ref, wmq_ref, wgate_ref,
                    gq_ref, gk_ref, gv_ref, gr_ref, la_ref, mq_ref, gate_ref):
    u = _rms(h_ref[...], g_ref[...]).astype(BF16)
    zg = _dot(u, wg_ref[...])
    gq_ref[...] = zg[:, :GLA_QW] * (GLA_DK ** -0.5)
    gk_ref[...] = zg[:, GLA_QW:2 * GLA_QW]
    gv_ref[...] = zg[:, 2 * GLA_QW:2 * GLA_QW + GLA_VW]
    gr_ref[...] = zg[:, 2 * GLA_QW + GLA_VW:]
    za = _dot(u, wga_ref[...]).astype(BF16)
    la_ref[...] = _log_sigmoid(_dot(za, wa2_ref[...]) + ba_ref[...]) / GLA_TAU
    mq_ref[...] = (_dot(u, wmq_ref[...]) * (HEAD_DIM ** -0.5)).astype(BF16)
    for c in range(N_BRANCH):
        sl = slice(c * D_MODEL, (c + 1) * D_MODEL)
        gate_ref[:, sl] = jax.nn.sigmoid(_dot(u, wgate_ref[:, sl])).astype(BF16)


def _proj_misc(h, g, wg, wga, wa2, ba, wmq, wgate):
    n = h.shape[0]
    tm = min(TOKEN_TILE, n)
    row = lambda w: pl.BlockSpec((tm, w), lambda i: (i, 0))
    outs = [(GLA_QW, F32), (GLA_QW, F32), (GLA_VW, F32), (GLA_VW, F32), (GLA_QW, F32),
            (BRANCH_W, BF16), (N_BRANCH * D_MODEL, BF16)]
    weights = [g, wg, wga, wa2, ba, wmq, wgate]
    return pl.pallas_call(
        _proj_misc_body,
        grid=(n // tm,),
        in_specs=[row(D_MODEL)] + [_resident(w.shape) for w in weights],
        out_specs=[row(w) for w, _ in outs],
        out_shape=[jax.ShapeDtypeStruct((n, w), d) for w, d in outs],
        compiler_params=_params(
            ("parallel",),
            pipelined=[((tm, D_MODEL), F32)] + [((tm, w), d) for w, d in outs],
            resident=[(w.shape, w.dtype) for w in weights]),
        name="proj_misc",
    )(h, *weights)


def _swa_prompt_body(*refs, seq):
    q_refs, k_refs, v_refs = refs[0:3], refs[3:6], refs[6:9]
    o_ref, oacc_ref, lacc_ref = refs[9:12]
    blk = SWA_BAND
    qi = lax.broadcasted_iota(jnp.int32, (blk, blk), 0)
    kj = lax.broadcasted_iota(jnp.int32, (blk, blk), 1)
    own_ok = kj <= qi
    prev_ok = kj >= qi

    for g, (_, dil) in enumerate(SWA_GROUPS):
        q_ref, k_ref, v_ref = q_refs[g], k_refs[g], v_refs[g]
        n_blk = seq // dil // blk

        def rows(start, dil=dil):
            if dil == 1:
                return pl.ds(start, blk)
            return pl.ds(start, blk, stride=dil)

        def block(start, has_prev, g=g, dil=dil, q_ref=q_ref, k_ref=k_ref, v_ref=v_ref, rows=rows):
            q = q_ref[0, rows(start), :].astype(BF16)
            k1 = k_ref[0, rows(start), :].astype(BF16)
            v1 = v_ref[0, rows(start), :].astype(BF16)
            s1 = jnp.where(own_ok, lax.dot_general(q, k1, _NT, preferred_element_type=F32), NEG)
            m = jnp.max(s1, axis=-1, keepdims=True)
            if has_prev:
                k0 = k_ref[0, rows(start - blk * dil), :].astype(BF16)
                v0 = v_ref[0, rows(start - blk * dil), :].astype(BF16)
                s0 = jnp.where(prev_ok, lax.dot_general(q, k0, _NT, preferred_element_type=F32), NEG)
                m = jnp.maximum(m, jnp.max(s0, axis=-1, keepdims=True))
            p1 = jnp.exp(s1 - m)
            l = jnp.sum(p1, axis=-1, keepdims=True)
            o = _dot(p1.astype(BF16), v1)
            if has_prev:
                p0 = jnp.exp(s0 - m)
                l = l + jnp.sum(p0, axis=-1, keepdims=True)
                o = o + _dot(p0.astype(BF16), v0)
            oacc_ref[g, rows(start), :] = o / l
            lacc_ref[g, rows(start), :] = jnp.broadcast_to(m + jnp.log(l), (blk, HEAD_DIM))

        def residue(r, carry, block=block, n_blk=n_blk, dil=dil):
            block(r, False)
            if n_blk > 1:
                def nxt(n, c):
                    block(r + n * blk * dil, True)
                    return c
                lax.fori_loop(1, n_blk, nxt, 0)
            return carry

        if dil == 1:
            residue(0, 0)
        else:
            lax.fori_loop(0, dil, residue, 0)

    l0, l1, l2 = lacc_ref[0], lacc_ref[1], lacc_ref[2]
    mx = jnp.maximum(jnp.maximum(l0, l1), l2)
    w0, w1, w2 = jnp.exp(l0 - mx), jnp.exp(l1 - mx), jnp.exp(l2 - mx)
    num = w0 * oacc_ref[0] + w1 * oacc_ref[1] + w2 * oacc_ref[2]
    o_ref[0] = (num / (w0 + w1 + w2)).astype(BF16)


def _swa_prompt(slabs, batch, seq):
    def spec(qkv, g):
        base = (qkv * N_SWA + g) * SWA_HEADS
        return pl.BlockSpec((1, seq, HEAD_DIM), lambda b, h: (base + h, b, 0))
    in_specs = [spec(qkv, g) for qkv in range(3) for g in range(N_SWA)]
    scratch = [((N_SWA, seq, HEAD_DIM), F32)] * 2
    return pl.pallas_call(
        functools.partial(_swa_prompt_body, seq=seq),
        grid=(batch, SWA_HEADS),
        in_specs=in_specs,
        out_specs=pl.BlockSpec((1, seq, HEAD_DIM), lambda b, h: (b, 0, h)),
        out_shape=jax.ShapeDtypeStruct((batch, seq, BRANCH_W), BF16),
        scratch_shapes=[pltpu.VMEM(s, d) for s, d in scratch],
        compiler_params=_params(
            ("parallel", "parallel"),
            pipelined=[((1, seq, HEAD_DIM), F32)] * 9 + [((1, seq, HEAD_DIM), BF16)],
            scratch=scratch),
        name="swa_prompt",
    )(*([slabs] * 9))


def _gla_body(*refs, seq, chunk, has_state):
    if has_state:
        q_ref, k_ref, la_ref, v_ref, r_ref, gn_ref, s0_ref, o_ref, s_ref, st_ref = refs
    else:
        q_ref, k_ref, la_ref, v_ref, r_ref, gn_ref, o_ref, s_ref, st_ref = refs
    cp = max(chunk, SUBLANES)
    n_chunks = seq // chunk
    H, DK, DV = GLA_HEADS, GLA_DK, GLA_DV

    if has_state:
        eye = (lax.broadcasted_iota(jnp.int32, (DV, DV), 0) ==
               lax.broadcasted_iota(jnp.int32, (DV, DV), 1)).astype(F32)
        st_ref[...] = jnp.zeros_like(st_ref)
        for h in range(H):
            st_ref[h * DV:(h + 1) * DV, h * DK:(h + 1) * DK] = lax.dot_general(
                eye, s0_ref[0, h], _NT, preferred_element_type=F32, precision=lax.Precision.HIGHEST)
    else:
        st_ref[...] = jnp.zeros_like(st_ref)

    def block_id(shape, axis, size):
        return lax.shift_right_logical(lax.broadcasted_iota(jnp.int32, shape, axis), int(math.log2(size)))

    ri = lax.broadcasted_iota(jnp.int32, (cp, cp), 0)
    ci = lax.broadcasted_iota(jnp.int32, (cp, cp), 1)
    tri = (ri >= ci).astype(F32)
    ai = lax.broadcasted_iota(jnp.int32, (cp, H * cp), 0)
    aj = lax.broadcasted_iota(jnp.int32, (cp, H * cp), 1) & (cp - 1)
    causal = aj <= ai
    k_diag = block_id((H * cp, H * DK), 0, cp) == block_id((H * cp, H * DK), 1, DK)
    v_diag = block_id((H * cp, H * DV), 0, cp) == block_id((H * cp, H * DV), 1, DV)
    s_diag = block_id((H * DV, H * DK), 0, DV) == block_id((H * DV, H * DK), 1, DK)
    gn = gn_ref[...]

    def pad(x):
        if cp == chunk:
            return x
        return jnp.concatenate([x, jnp.zeros((cp - chunk, x.shape[1]), x.dtype)], axis=0)

    def step(c, carry):
        r0 = c * chunk
        if not isinstance(r0, int):
            r0 = pl.multiple_of(r0, chunk)
        rows = pl.ds(r0, chunk)
        q = pad(q_ref[0, rows, :])
        k = pad(k_ref[0, rows, :])
        la = pad(la_ref[0, rows, :])
        v = pad(v_ref[0, rows, :])
        b = jnp.dot(tri, la, preferred_element_type=F32, precision=lax.Precision.HIGHEST)
        b_end = b[cp - 1:cp, :]
        qe = (q * jnp.exp(b)).astype(BF16)
        ke = (k * jnp.exp(-b)).astype(BF16)
        kd = (k * jnp.exp(b_end - b)).astype(BF16)
        vb = v.astype(BF16)
        k_bd = jnp.where(k_diag, jnp.concatenate([ke] * H, axis=0), jnp.zeros((), BF16))
        att = lax.dot_general(qe, k_bd, _NT, preferred_element_type=F32)
        att = jnp.where(causal, att, 0.0).astype(BF16)
        v_bd = jnp.where(v_diag, jnp.concatenate([vb] * H, axis=0), jnp.zeros((), BF16))
        st = st_ref[...]
        o = _dot(att, v_bd) + lax.dot_general(qe, st.astype(BF16), _NT, preferred_element_type=F32)
        upd = lax.dot_general(vb, kd, _TN, preferred_element_type=F32)
        st_ref[...] = st * jnp.exp(b_end) + jnp.where(s_diag, upd, 0.0)
        gr = r_ref[0, rows, :]
        for h in range(H):
            oh = o[:chunk, h * DV:(h + 1) * DV]
            gh = gr[:, h * DV:(h + 1) * DV]
            o_ref[0, rows, h * DV:(h + 1) * DV] = (_rms(oh, gn) * (gh * jax.nn.sigmoid(gh))).astype(BF16)
        return carry

    if n_chunks == 1:
        step(0, 0)
    else:
        lax.fori_loop(0, n_chunks, step, 0)

    eye_k = (lax.broadcasted_iota(jnp.int32, (DK, DK), 0) ==
             lax.broadcasted_iota(jnp.int32, (DK, DK), 1)).astype(F32)
    for h in range(H):
        s_ref[0, h] = lax.dot_general(
            eye_k, st_ref[h * DV:(h + 1) * DV, h * DK:(h + 1) * DK], _NT,
            preferred_element_type=F32, precision=lax.Precision.HIGHEST)


def _gla(gq, gk, la, gv, gr, gn, s0, batch, seq):
    chunk = GLA_CHUNK if seq % GLA_CHUNK == 0 else seq
    has_state = s0 is not None
    row = lambda w: pl.BlockSpec((1, seq, w), lambda b: (b, 0, 0))
    st_spec = pl.BlockSpec((1, GLA_HEADS, GLA_DK, GLA_DV), lambda b: (b, 0, 0, 0))
    args = [gq, gk, la, gv, gr, gn] + ([s0] if has_state else [])
    in_specs = [row(GLA_QW)] * 3 + [row(GLA_VW)] * 2 + [_resident(gn.shape)] + ([st_spec] if has_state else [])
    scratch = [((GLA_VW, GLA_QW), F32)]
    return pl.pallas_call(
        functools.partial(_gla_body, seq=seq, chunk=chunk, has_state=has_state),
        grid=(batch,),
        in_specs=in_specs,
        out_specs=[row(GLA_VW), st_spec],
        out_shape=[jax.ShapeDtypeStruct((batch, seq, GLA_VW), BF16),
                   jax.ShapeDtypeStruct((batch, GLA_HEADS, GLA_DK, GLA_DV), F32)],
        scratch_shapes=[pltpu.VMEM(s, d) for s, d in scratch],
        compiler_params=_params(
            ("parallel",),
            pipelined=[((seq, GLA_QW), F32)] * 3 + [((seq, GLA_VW), F32)] * 2 + [((seq, GLA_VW), BF16)]
            + [((GLA_HEADS, GLA_DK, GLA_DV), F32)] * 2,
            scratch=scratch),
        name="gla_state" if has_state else "gla",
    )(*args)


MEM_Q_CHUNK = 512


def _mem_prompt_body(mem_ref, g_ref, w_ref, q_ref, kv_ref, o_ref, *, seq):
    kv = _dot(_rms(mem_ref[0], g_ref[...]).astype(BF16), w_ref[...])
    kv_ref[0] = kv
    hw = MEM_HEADS * HEAD_DIM
    for h in range(MEM_HEADS):
        k = kv[:, h * HEAD_DIM:(h + 1) * HEAD_DIM].astype(BF16)
        v = kv[:, hw + h * HEAD_DIM:hw + (h + 1) * HEAD_DIM].astype(BF16)
        for c in range(seq // MEM_Q_CHUNK):
            rows = slice(c * MEM_Q_CHUNK, (c + 1) * MEM_Q_CHUNK)
            q = q_ref[0, rows, h * HEAD_DIM:(h + 1) * HEAD_DIM]
            s = lax.dot_general(q, k, _NT, preferred_element_type=F32)
            p = jnp.exp(s - jnp.max(s, axis=-1, keepdims=True))
            l = jnp.sum(p, axis=-1, keepdims=True)
            o_ref[0, rows, h * HEAD_DIM:(h + 1) * HEAD_DIM] = (_dot(p.astype(BF16), v) / l).astype(BF16)


def _mem_prompt(mem, g, w, mq, batch, seq):
    kvw = 2 * MEM_HEADS * HEAD_DIM
    return pl.pallas_call(
        functools.partial(_mem_prompt_body, seq=seq),
        grid=(batch,),
        in_specs=[pl.BlockSpec((1, MEM_LEN, D_MODEL), lambda b: (b, 0, 0)), _resident(g.shape),
                  _resident(w.shape), pl.BlockSpec((1, seq, BRANCH_W), lambda b: (b, 0, 0))],
        out_specs=[pl.BlockSpec((1, MEM_LEN, kvw), lambda b: (b, 0, 0)),
                   pl.BlockSpec((1, seq, BRANCH_W), lambda b: (b, 0, 0))],
        out_shape=[jax.ShapeDtypeStruct((batch, MEM_LEN, kvw), F32),
                   jax.ShapeDtypeStruct((batch, seq, BRANCH_W), BF16)],
        compiler_params=_params(
            ("parallel",),
            pipelined=[((MEM_LEN, D_MODEL), F32), ((seq, BRANCH_W), BF16), ((MEM_LEN, kvw), F32),
                       ((seq, BRANCH_W), BF16)],
            resident=[(g.shape, g.dtype), (w.shape, w.dtype)]),
        name="mem_prompt",
    )(mem, g, w, mq)


def _sample_attn_body(q_ref, xn_ref, c0_ref, c1_ref, c2_ref, cm_ref, b0_ref, b1_ref, b2_ref, bm_ref,
                      bn_ref, oswa_ref, omem_ref):
    half = SWA_HEADS

    def kv_swapped(x):
        x3 = x.reshape(x.shape[0] // SUBLANES, SUBLANES, LANES)
        return pltpu.roll(x3, half, axis=1).reshape(x.shape)

    def attend(q, x, bias, xn=None, bias_n=None):
        xk = x.astype(BF16)
        xv = kv_swapped(x).astype(BF16)
        s = lax.dot_general(q, xk, _NT, preferred_element_type=F32) + bias
        m = jnp.max(s, axis=-1, keepdims=True)
        if xn is not None:
            nk = xn.astype(BF16)
            nv = kv_swapped(xn).astype(BF16)
            sn = lax.dot_general(q, nk, _NT, preferred_element_type=F32) + bias_n
            m = jnp.maximum(m, jnp.max(sn, axis=-1, keepdims=True))
        p = jnp.exp(s - m)
        l = jnp.sum(p, axis=-1, keepdims=True)
        o = _dot(p.astype(BF16), xv)
        if xn is not None:
            pn = jnp.exp(sn - m)
            l = l + jnp.sum(pn, axis=-1, keepdims=True)
            o = o + _dot(pn.astype(BF16), nv)
        return o / l, m + jnp.log(l)

    caches = (c0_ref[0], c1_ref[0].reshape(-1, LANES), c2_ref[0].reshape(-1, LANES))
    biases = (b0_ref, b1_ref, b2_ref)
    outs, lses = [], []
    for g in range(N_SWA):
        o, lse = attend(q_ref[0, g], caches[g], biases[g][...], xn_ref[0, g], bn_ref[g])
        outs.append(o)
        lses.append(lse)
    mx = jnp.maximum(jnp.maximum(lses[0], lses[1]), lses[2])
    ws = [jnp.exp(l - mx) for l in lses]
    num = ws[0] * outs[0] + ws[1] * outs[1] + ws[2] * outs[2]
    oswa_ref[0] = (num / (ws[0] + ws[1] + ws[2])).astype(BF16)
    om, _ = attend(q_ref[0, N_SWA], cm_ref[0], bm_ref[...])
    omem_ref[0] = om.astype(BF16)


def _sample_bias_tables(dec_seq):
    rows_q = dec_seq * SWA_HEADS
    t = (np.arange(rows_q) // SWA_HEADS)[:, None]
    hq = (np.arange(rows_q) % SWA_HEADS)[:, None]

    def table(ok):
        return np.where(ok, 0.0, NEG).astype(np.float32)

    cache_tabs, new_tabs = [], []
    for win, dil in SWA_GROUPS:
        length = min(win, PAST_LEN)
        res = min(dil, dec_seq)
        r = np.arange((length // dil) * res * 2 * SWA_HEADS)[None, :]
        m, rr, kv, hl = r // (res * 8), (r // 8) % res, (r // 4) % 2, r % 4
        delta = length + t - (m * dil + rr)
        ok = (kv == 0) & (hl == hq) & (delta % dil == 0) & (delta // dil >= 0) & (delta // dil <= SWA_BAND)
        cache_tabs.append(table(ok))
        rn = np.arange(dec_seq * 2 * SWA_HEADS)[None, :]
        tn, kvn, hn = rn // 8, (rn // 4) % 2, rn % 4
        dn = t - tn
        okn = (kvn == 0) & (hn == hq) & (dn % dil == 0) & (dn // dil >= 0) & (dn // dil <= SWA_BAND)
        new_tabs.append(table(okn))
    rm = np.arange(MEM_LEN * 2 * MEM_HEADS)[None, :]
    mem_tab = table(((rm // 4) % 2 == 0) & (rm % 4 == hq))
    return cache_tabs, np.stack(new_tabs), mem_tab


def _sample_attn(qall, xnew, cache_swa, cache_mem, dec_batch, dec_seq):
    rows_q = dec_seq * SWA_HEADS
    cache_tabs, new_tab, mem_tab = _sample_bias_tables(dec_seq)
    kvh = 2 * SWA_HEADS
    views, specs, blocks = [], [], []
    for (win, dil), c in zip(SWA_GROUPS, cache_swa):
        length = min(win, PAST_LEN)
        res = min(dil, dec_seq)
        if dil == 1:
            v = c.reshape(dec_batch, length * kvh, LANES)
            blk = (1, length * kvh, LANES)
            specs.append(pl.BlockSpec(blk, lambda b: (b, 0, 0)))
        else:
            v = c.reshape(dec_batch, length // dil, dil * kvh, LANES)
            blk = (1, length // dil, res * kvh, LANES)
            specs.append(pl.BlockSpec(blk, lambda b: (b, 0, 0, 0)))
        views.append(v)
        blocks.append((blk, F32))
    cm = cache_mem.reshape(dec_batch, MEM_LEN * kvh, LANES)
    cm_blk = (1, MEM_LEN * kvh, LANES)
    consts = [jnp.asarray(tb) for tb in cache_tabs] + [jnp.asarray(mem_tab), jnp.asarray(new_tab)]
    out_blk = pl.BlockSpec((1, rows_q, HEAD_DIM), lambda b: (b, 0, 0))
    return pl.pallas_call(
        _sample_attn_body,
        grid=(dec_batch,),
        in_specs=[pl.BlockSpec((1, N_SWA + 1, rows_q, HEAD_DIM), lambda b: (b, 0, 0, 0)),
                  pl.BlockSpec((1, N_SWA, dec_seq * kvh, HEAD_DIM), lambda b: (b, 0, 0, 0))]
        + specs + [pl.BlockSpec(cm_blk, lambda b: (b, 0, 0))] + [_resident(c.shape) for c in consts],
        out_specs=[out_blk, out_blk],
        out_shape=[jax.ShapeDtypeStruct((dec_batch, rows_q, HEAD_DIM), BF16)] * 2,
        compiler_params=_params(
            ("parallel",),
            pipelined=blocks + [(cm_blk, F32)],
            resident=[(c.shape, c.dtype) for c in consts]),
        name="sample_attn",
    )(qall, xnew, *views, cm, *consts)


def _merge_body(h_ref, gate_ref, oa_ref, ob_ref, oc_ref, wbr_ref, wo_ref, o_ref):
    mix = None
    for n, o_b in enumerate((oa_ref, ob_ref, oc_ref)):
        br = _dot(o_b[...], wbr_ref[n])
        term = gate_ref[:, n * D_MODEL:(n + 1) * D_MODEL].astype(F32) * br
        mix = term if mix is None else mix + term
    o_ref[...] = h_ref[...] + _dot(mix.astype(BF16), wo_ref[...])


def _merge(h, gates, o_swa, o_gla, o_mem, wbr, wo):
    n = h.shape[0]
    tm = min(TOKEN_TILE, n)
    row = lambda w: pl.BlockSpec((tm, w), lambda i: (i, 0))
    return pl.pallas_call(
        _merge_body,
        grid=(n // tm,),
        in_specs=[row(D_MODEL), row(N_BRANCH * D_MODEL), row(BRANCH_W), row(BRANCH_W), row(BRANCH_W),
                  _resident(wbr.shape), _resident(wo.shape)],
        out_specs=row(D_MODEL),
        out_shape=jax.ShapeDtypeStruct((n, D_MODEL), F32),
        compiler_params=_params(
            ("parallel",),
            pipelined=[((tm, D_MODEL), F32)] * 2 + [((tm, N_BRANCH * D_MODEL), BF16)]
            + [((tm, BRANCH_W), BF16)] * 3,
            resident=[(wbr.shape, wbr.dtype), (wo.shape, wo.dtype)]),
        name="merge",
    )(h, gates, o_swa, o_gla, o_mem, wbr, wo)


def _rope_tables(positions):
    half = HEAD_DIM // 2
    inv = ROPE_THETA ** (-np.arange(half, dtype=np.float64) / half)
    ang = np.asarray(positions, np.float64)[:, None] * inv[None, :]
    cos, sin = np.cos(ang), np.sin(ang)
    return (jnp.asarray(np.concatenate([cos, cos], axis=1), F32),
            jnp.asarray(np.concatenate([-sin, sin], axis=1), F32))


def kernel(x_prompt, x_sample, mem_prompt, cache_swa0, cache_swa1, cache_swa2, cache_mem_kv, state_gla,
           g_ffn1, w_ffn1_in, w_ffn1_out, g_mix, w_in, w_gla_a2, b_gla_a, g_gla_out, w_gate, w_branch,
           w_out, g_mem, w_mem_kv, g_ffn2, w_ffn2_in, w_ffn2_out, g_final):
    batch, seq, _ = x_prompt.shape
    dec_batch, dec_seq, _ = x_sample.shape
    depth = g_ffn1.shape[0]
    assert depth == 1, "single-layer step"
    l = 0
    row = lambda g: g.reshape(1, -1)

    w1i, w2i = w_ffn1_in[l].astype(BF16), w_ffn2_in[l].astype(BF16)
    w1a, w1b, w1o = w1i[:, :D_FF], w1i[:, D_FF:], w_ffn1_out[l].astype(BF16)
    w2a, w2b, w2o = w2i[:, :D_FF], w2i[:, D_FF:], w_ffn2_out[l].astype(BF16)
    wi = w_in[l].astype(BF16)
    o_gla0 = 3 * SWA_W
    o_ga = o_gla0 + 2 * GLA_QW + 2 * GLA_VW
    w_swa = wi[:, :o_gla0]
    w_g = wi[:, o_gla0:o_ga]
    w_ga = jnp.pad(wi[:, o_ga:o_ga + GLA_RANK], ((0, 0), (0, LANES - GLA_RANK)))
    w_a2 = jnp.pad(w_gla_a2[l].astype(BF16), ((0, LANES - GLA_RANK), (0, 0)))
    w_mq = wi[:, o_ga + GLA_RANK:]
    wgate = w_gate[l].astype(BF16)
    wbr = w_branch[l].astype(BF16)
    wo = w_out[l].astype(BF16)
    wmem = w_mem_kv[l].astype(BF16)
    weights_misc = (row(g_mix[l]), w_g, w_ga, w_a2, row(b_gla_a[l]), w_mq, wgate)

    n_p = batch * seq
    cos_p, sin_p = _rope_tables(np.arange(seq))
    hp = _ffn(x_prompt.reshape(n_p, D_MODEL), row(g_ffn1[l]), w1a, w1b, w1o)
    slabs = _proj_swa(hp, row(g_mix[l]), cos_p, sin_p, w_swa, seq // min(TOKEN_TILE, n_p))
    gq, gk, gv, gr, la, mq, gates = _proj_misc(hp, *weights_misc)
    o_swa = _swa_prompt(slabs, batch, seq)
    r3 = lambda a: a.reshape(batch, seq, a.shape[-1])
    o_gla, gla_p = _gla(r3(gq), r3(gk), r3(la), r3(gv), r3(gr), row(g_gla_out[l]), None, batch, seq)
    kv_p, o_mem = _mem_prompt(mem_prompt, row(g_mem[l]), wmem, r3(mq), batch, seq)
    hp = _merge(hp, gates, o_swa.reshape(n_p, BRANCH_W), o_gla.reshape(n_p, BRANCH_W),
                o_mem.reshape(n_p, BRANCH_W), wbr, wo)
    y_prompt = _ffn(hp, row(g_ffn2[l]), w2a, w2b, w2o, row(g_final)).reshape(batch, seq, D_MODEL)

    kv5 = slabs[N_SWA * SWA_HEADS:].reshape(2, N_SWA, SWA_HEADS, batch, seq, HEAD_DIM)
    swa_p = []
    for g, (win, _) in enumerate(SWA_GROUPS):
        keep = min(win, seq)
        swa_p.append(jnp.transpose(kv5[:, g, :, :, seq - keep:], (2, 3, 0, 1, 4))[None])
    mem_kv_p = kv_p.reshape(1, batch, MEM_LEN, 2, MEM_HEADS, HEAD_DIM)

    n_s = dec_batch * dec_seq
    cos_s, sin_s = _rope_tables(PAST_LEN + (np.arange(n_s) % dec_seq))
    hs = _ffn(x_sample.reshape(n_s, D_MODEL), row(g_ffn1[l]), w1a, w1b, w1o)
    slabs_s = _proj_swa(hs, row(g_mix[l]), cos_s, sin_s, w_swa, 1)
    gq, gk, gv, gr, la, mq, gates = _proj_misc(hs, *weights_misc)
    s3 = lambda a: a.reshape(dec_batch, dec_seq, a.shape[-1])
    o_gla_s, gla_s = _gla(s3(gq), s3(gk), s3(la), s3(gv), s3(gr), row(g_gla_out[l]), state_gla[l],
                          dec_batch, dec_seq)
    s6 = slabs_s.reshape(3, N_SWA, SWA_HEADS, dec_batch, dec_seq, HEAD_DIM)
    q_swa = jnp.transpose(s6[0], (2, 0, 3, 1, 4)).reshape(dec_batch, N_SWA, dec_seq * SWA_HEADS, HEAD_DIM)
    q_mem = mq.reshape(dec_batch, 1, dec_seq * MEM_HEADS, HEAD_DIM)
    qall = jnp.concatenate([q_swa.astype(BF16), q_mem], axis=1)
    xnew = jnp.transpose(s6[1:], (3, 1, 4, 0, 2, 5))
    o_swa_s, o_mem_s = _sample_attn(
        qall, xnew.reshape(dec_batch, N_SWA, dec_seq * 2 * SWA_HEADS, HEAD_DIM),
        (cache_swa0[l], cache_swa1[l], cache_swa2[l]), cache_mem_kv[l], dec_batch, dec_seq)
    hs = _merge(hs, gates, o_swa_s.reshape(n_s, BRANCH_W), o_gla_s.reshape(n_s, BRANCH_W),
                o_mem_s.reshape(n_s, BRANCH_W), wbr, wo)
    y_sample = _ffn(hs, row(g_ffn2[l]), w2a, w2b, w2o, row(g_final)).reshape(dec_batch, dec_seq, D_MODEL)
    swa_s = [xnew[:, g][None] for g in range(N_SWA)]

    return (y_prompt, y_sample, swa_p[0], swa_p[1], swa_p[2], mem_kv_p, gla_p[None],
            swa_s[0], swa_s[1], swa_s[2], gla_s[None])
```

```python
import functools
import math

import numpy as np
import jax
import jax.numpy as jnp
from jax import lax
from jax.experimental import pallas as pl
from jax.experimental.pallas import tpu as pltpu

F32 = jnp.float32
BF16 = jnp.bfloat16

D_MODEL = 1024
PAST_LEN = 8192
HEAD_DIM = 128
SWA_GROUPS = ((128, 1), (512, 4), (2048, 16))
N_SWA = 3
SWA_HEADS = 4
SWA_BAND = 128
GLA_HEADS = 4
GLA_DK = 64
GLA_DV = 128
GLA_RANK = 16
GLA_TAU = 16.0
GLA_CHUNK = 64
MEM_LEN = 256
MEM_HEADS = 4
BRANCH_W = 512
N_BRANCH = 3
D_FF = 2816
ROPE_THETA = 10000.0
EPS = 1e-6
NEG = -1e30

SWA_W = N_SWA * SWA_HEADS * HEAD_DIM
GLA_QW = GLA_HEADS * GLA_DK
GLA_VW = GLA_HEADS * GLA_DV

LANES = 128
SUBLANES = 8
MXU_DIM = 256
VMEM_BYTES_V7X = 64 * 1024 * 1024
VMEM_INTERNAL_RESERVE = 10 * 1024 * 1024

TOKEN_TILE = 512
FF_CHUNK = MXU_DIM

_NT = (((1,), (1,)), ((), ()))
_TN = (((0,), (0,)), ((), ()))


def _nbytes(shape, dtype):
    return math.prod(shape) * jnp.dtype(dtype).itemsize


def _params(semantics, pipelined, resident=(), scratch=()):
    need = 2 * sum(_nbytes(s, d) for s, d in pipelined)
    need += sum(_nbytes(s, d) for s, d in resident)
    need += sum(_nbytes(s, d) for s, d in scratch)
    limit = min(need + VMEM_INTERNAL_RESERVE, VMEM_BYTES_V7X - 4 * 1024 * 1024)
    return pltpu.CompilerParams(dimension_semantics=semantics, vmem_limit_bytes=limit)


def _resident(shape):
    del shape
    return pl.BlockSpec(memory_space=pltpu.VMEM)


def _rms(x, g):
    return x * lax.rsqrt(jnp.mean(x * x, axis=-1, keepdims=True) + EPS) * g


def _dot(a, b):
    return jnp.dot(a, b, preferred_element_type=F32)


def _ffn_body(*refs, final):
    if final:
        x_ref, g_ref, wa_ref, wb_ref, wo_ref, gf_ref, o_ref, hid_ref = refs
    else:
        x_ref, g_ref, wa_ref, wb_ref, wo_ref, o_ref, hid_ref = refs
    x = x_ref[...]
    xb = _rms(x, g_ref[...]).astype(BF16)
    for c in range(D_FF // FF_CHUNK):
        sl = slice(c * FF_CHUNK, (c + 1) * FF_CHUNK)
        a = _dot(xb, wa_ref[:, sl])
        b = _dot(xb, wb_ref[:, sl])
        hid_ref[:, sl] = (a * jax.nn.sigmoid(a) * b).astype(BF16)
    y = x + 0.5 * _dot(hid_ref[...], wo_ref[...])
    if final:
        y = _rms(y, gf_ref[...])
    o_ref[...] = y


def _ffn(x, g, wa, wb, wo, g_final=None):
    n = x.shape[0]
    tm = min(TOKEN_TILE, n)
    final = g_final is not None
    tile = pl.BlockSpec((tm, D_MODEL), lambda i: (i, 0))
    args = [x, g, wa, wb, wo] + ([g_final] if final else [])
    in_specs = [tile] + [_resident(a.shape) for a in args[1:]]
    return pl.pallas_call(
        functools.partial(_ffn_body, final=final),
        grid=(n // tm,),
        in_specs=in_specs,
        out_specs=tile,
        out_shape=jax.ShapeDtypeStruct((n, D_MODEL), F32),
        scratch_shapes=[pltpu.VMEM((tm, D_FF), BF16)],
        compiler_params=_params(
            ("parallel",),
            pipelined=[((tm, D_MODEL), F32)] * 2,
            resident=[(a.shape, a.dtype) for a in args[1:]],
            scratch=[((tm, D_FF), BF16)]),
        name="ffn_final" if final else "ffn",
    )(*args)


KV_ROWS = 2 * SWA_HEADS
QKV_SLABS = 3 * SWA_HEADS


def _proj_swa_body(*refs, seq, tm, prompt):
    if prompt:
        h_ref, g_ref, cos_ref, sin_ref, w_ref = refs[:5]
        slab_refs, row_refs = refs[5:8], refs[8:11]
        stage_ref, tail_ref = refs[11:13]
        last_tile = pl.program_id(0) % (seq // tm) == seq // tm - 1
    else:
        h_ref, g_ref, cos_ref, sin_ref, w_ref, o_ref = refs
    u = _rms(h_ref[...], g_ref[...]).astype(BF16)
    cos = cos_ref[...]
    sin = sin_ref[...]
    for c in range(3 * N_SWA):
        qkv, g = divmod(c, N_SWA)
        win, dil = SWA_GROUPS[g]
        keep = min(win, seq)
        z = _dot(u, w_ref[:, c * BRANCH_W:(c + 1) * BRANCH_W])
        for j in range(SWA_HEADS):
            s = z[:, j * HEAD_DIM:(j + 1) * HEAD_DIM]
            if qkv < 2:
                s = s * cos + pltpu.roll(s, HEAD_DIM // 2, axis=1) * sin
            if qkv == 0:
                s = s * (HEAD_DIM ** -0.5)
            slab = qkv * SWA_HEADS + j
            if not prompt:
                o_ref[c * SWA_HEADS + j] = s
                continue
            if dil == 1:
                slab_refs[g][slab] = s.astype(BF16)
                if qkv > 0 and keep < seq:
                    tail_ref[slab - SWA_HEADS] = s[tm - keep:, :]
            else:
                k = (g - 1) * QKV_SLABS + slab
                stage_ref[k] = s
                for r in range(dil):
                    slab_refs[g][slab, 0, r] = stage_ref[k, pl.ds(r, tm // dil, stride=dil), :].astype(BF16)
            if qkv > 0 and keep == seq:
                row_refs[g][pl.ds(slab - SWA_HEADS, tm, stride=KV_ROWS), :] = s
    if prompt:
        @pl.when(last_tile)
        def _():
            for g, (win, dil) in enumerate(SWA_GROUPS):
                keep = min(win, seq)
                if keep == seq:
                    continue
                for c_row in range(KV_ROWS):
                    if dil == 1:
                        src = tail_ref[c_row]
                    else:
                        src = stage_ref[(g - 1) * QKV_SLABS + SWA_HEADS + c_row, tm - keep:, :]
                    row_refs[g][pl.ds(c_row, keep, stride=KV_ROWS), :] = src


def _proj_swa_sample(h, g, cos, sin, w):
    n = h.shape[0]
    n_slabs = N_SWA * QKV_SLABS
    return pl.pallas_call(
        functools.partial(_proj_swa_body, seq=n, tm=n, prompt=False),
        grid=(1,),
        in_specs=[pl.BlockSpec((n, D_MODEL), lambda i: (0, 0)), _resident(g.shape),
                  pl.BlockSpec((n, HEAD_DIM), lambda i: (0, 0)), pl.BlockSpec((n, HEAD_DIM), lambda i: (0, 0)),
                  _resident(w.shape)],
        out_specs=pl.BlockSpec((n_slabs, n, HEAD_DIM), lambda i: (0, 0, 0)),
        out_shape=jax.ShapeDtypeStruct((n_slabs, n, HEAD_DIM), F32),
        compiler_params=_params(
            ("arbitrary",),
            pipelined=[((n, D_MODEL), F32), ((n, HEAD_DIM), F32), ((n, HEAD_DIM), F32),
                       ((n_slabs, n, HEAD_DIM), F32)],
            resident=[(g.shape, g.dtype), (w.shape, w.dtype)]),
        name="proj_swa_sample",
    )(h, g, cos, sin, w)


def _proj_swa_prompt(h, g, cos, sin, w, batch, seq):
    n = batch * seq
    tm = TOKEN_TILE
    tps = seq // tm
    keeps = [min(win, seq) for win, _ in SWA_GROUPS]
    assert seq % tm == 0 and all(k == seq or k <= tm for k in keeps)
    assert all(tm % (dil * 2 * SUBLANES) == 0 for _, dil in SWA_GROUPS)
    tile = pl.BlockSpec((tm, D_MODEL), lambda i: (i, 0))
    pos = pl.BlockSpec((tm, HEAD_DIM), lambda i: (i % tps, 0))
    out_specs, out_shape, pipelined = [], [], []
    for _, dil in SWA_GROUPS:
        if dil == 1:
            shp, blk = (QKV_SLABS, n, HEAD_DIM), (QKV_SLABS, tm, HEAD_DIM)
            out_specs.append(pl.BlockSpec(blk, lambda i: (0, i, 0)))
        else:
            shp = (QKV_SLABS, batch, dil, seq // dil, HEAD_DIM)
            blk = (QKV_SLABS, 1, dil, tm // dil, HEAD_DIM)
            out_specs.append(pl.BlockSpec(blk, lambda i: (0, i // tps, 0, i % tps, 0)))
        out_shape.append(jax.ShapeDtypeStruct(shp, BF16))
        pipelined.append((blk, BF16))
    for keep in keeps:
        if keep == seq:
            blk = (tm * KV_ROWS, LANES)
            out_specs.append(pl.BlockSpec(blk, lambda i: (i, 0)))
        else:
            blk = (keep * KV_ROWS, LANES)
            out_specs.append(pl.BlockSpec(blk, lambda i: (i // tps, 0)))
        out_shape.append(jax.ShapeDtypeStruct((batch * keep * KV_ROWS, LANES), F32))
        pipelined.append((blk, F32))
    n_dilated = sum(1 for _, dil in SWA_GROUPS if dil > 1)
    scratch = [((n_dilated * QKV_SLABS, tm, HEAD_DIM), F32), ((KV_ROWS, keeps[0], HEAD_DIM), F32)]
    return pl.pallas_call(
        functools.partial(_proj_swa_body, seq=seq, tm=tm, prompt=True),
        grid=(n // tm,),
        in_specs=[tile, _resident(g.shape), pos, pos, _resident(w.shape)],
        out_specs=out_specs,
        out_shape=out_shape,
        scratch_shapes=[pltpu.VMEM(s, d) for s, d in scratch],
        compiler_params=_params(
            ("arbitrary",),
            pipelined=[((tm, D_MODEL), F32), ((tm, HEAD_DIM), F32), ((tm, HEAD_DIM), F32)] + pipelined,
            resident=[(g.shape, g.dtype), (w.shape, w.dtype)],
            scratch=scratch),
        name="proj_swa",
    )(h, g, cos, sin, w)


def _log_sigmoid(x):
    return jnp.minimum(x, 0.0) - jnp.log1p(jnp.exp(-jnp.abs(x)))


def _proj_misc_body(h_ref, g_ref, wg_ref, wga_ref, wa2_ref, ba_ref, wmq_ref, wgate_ref,
                    gq_ref, gk_ref, gv_ref, gr_ref, la_ref, mq_ref, gate_ref):
    u = _rms(h_ref[...], g_ref[...]).astype(BF16)
    zg = _dot(u, wg_ref[...])
    gq_ref[...] = zg[:, :GLA_QW] * (GLA_DK ** -0.5)
    gk_ref[...] = zg[:, GLA_QW:2 * GLA_QW]
    gv_ref[...] = zg[:, 2 * GLA_QW:2 * GLA_QW + GLA_VW]
    gr_ref[...] = zg[:, 2 * GLA_QW + GLA_VW:]
    za = _dot(u, wga_ref[...]).astype(BF16)
    la_ref[...] = _log_sigmoid(_dot(za, wa2_ref[...]) + ba_ref[...]) / GLA_TAU
    mq_ref[...] = (_dot(u, wmq_ref[...]) * (HEAD_DIM ** -0.5)).astype(BF16)
    for c in range(N_BRANCH):
        sl = slice(c * D_MODEL, (c + 1) * D_MODEL)
        gate_ref[:, sl] = jax.nn.sigmoid(_dot(u, wgate_ref[:, sl])).astype(BF16)


def _proj_misc(h, g, wg, wga, wa2, ba, wmq, wgate):
    n = h.shape[0]
    tm = min(TOKEN_TILE, n)
    row = lambda w: pl.BlockSpec((tm, w), lambda i: (i, 0))
    outs = [(GLA_QW, F32), (GLA_QW, F32), (GLA_VW, F32), (GLA_VW, F32), (GLA_QW, F32),
            (BRANCH_W, BF16), (N_BRANCH * D_MODEL, BF16)]
    weights = [g, wg, wga, wa2, ba, wmq, wgate]
    return pl.pallas_call(
        _proj_misc_body,
        grid=(n // tm,),
        in_specs=[row(D_MODEL)] + [_resident(w.shape) for w in weights],
        out_specs=[row(w) for w, _ in outs],
        out_shape=[jax.ShapeDtypeStruct((n, w), d) for w, d in outs],
        compiler_params=_params(
            ("parallel",),
            pipelined=[((tm, D_MODEL), F32)] + [((tm, w), d) for w, d in outs],
            resident=[(w.shape, w.dtype) for w in weights]),
        name="proj_misc",
    )(h, *weights)


def _swa_prompt_body(*refs, seq):
    o_ref, oacc_ref, lacc_ref = refs[9:12]
    blk = SWA_BAND
    gap = (lax.broadcasted_iota(jnp.int32, (blk, 2 * blk), 1) -
           lax.broadcasted_iota(jnp.int32, (blk, 2 * blk), 0))
    ok_cat = (gap >= 0) & (gap <= blk)
    ok_own = (lax.broadcasted_iota(jnp.int32, (blk, blk), 1) <=
              lax.broadcasted_iota(jnp.int32, (blk, blk), 0))
    ones = jnp.ones((2 * blk, blk), BF16)

    def attend(q, kc, vc, ok):
        s = jnp.where(ok, lax.dot_general(q, kc, _NT, preferred_element_type=F32), NEG)
        m = jnp.max(s, axis=-1, keepdims=True)
        p = jnp.exp(s - m).astype(BF16)
        acc = _dot(p, jnp.concatenate([vc, ones[:vc.shape[0]]], axis=1))
        l = acc[:, blk:]
        return acc[:, :blk] / l, m + jnp.log(l)

    def put(g, rows, res):
        oacc_ref[g, rows, :] = res[0]
        lacc_ref[g, rows, :] = res[1]

    for g, (_, dil) in enumerate(SWA_GROUPS):
        q_ref, k_ref, v_ref = refs[3 * g:3 * g + 3]
        n_blk = seq // dil // blk
        if dil == 1:
            put(g, pl.ds(0, blk), attend(q_ref[0, :blk], k_ref[0, :blk], v_ref[0, :blk], ok_own))

            def nxt(n, c, g=g, q_ref=q_ref, k_ref=k_ref, v_ref=v_ref):
                r0 = pl.multiple_of(n * blk, blk)
                keys = pl.ds(r0 - blk, 2 * blk)
                put(g, pl.ds(r0, blk), attend(q_ref[0, pl.ds(r0, blk)], k_ref[0, keys], v_ref[0, keys], ok_cat))
                return c
            lax.fori_loop(1, n_blk, nxt, 0, unroll=3 if (n_blk - 1) % 3 == 0 else 1)
        else:
            def residue(r, c, g=g, dil=dil, n_blk=n_blk, q_ref=q_ref, k_ref=k_ref, v_ref=v_ref):
                for n in range(n_blk):
                    rows = pl.ds(r + n * blk * dil, blk, stride=dil)
                    q = q_ref[0, 0, r, n * blk:(n + 1) * blk]
                    if n == 0:
                        put(g, rows, attend(q, k_ref[0, 0, r, :blk], v_ref[0, 0, r, :blk], ok_own))
                    else:
                        keys = slice((n - 1) * blk, (n + 1) * blk)
                        put(g, rows, attend(q, k_ref[0, 0, r, keys], v_ref[0, 0, r, keys], ok_cat))
                return c
            lax.fori_loop(0, dil, residue, 0, unroll=max(1, 4 // n_blk))

    l0, l1, l2 = lacc_ref[0], lacc_ref[1], lacc_ref[2]
    mx = jnp.maximum(jnp.maximum(l0, l1), l2)
    w0, w1, w2 = jnp.exp(l0 - mx), jnp.exp(l1 - mx), jnp.exp(l2 - mx)
    num = w0 * oacc_ref[0] + w1 * oacc_ref[1] + w2 * oacc_ref[2]
    o_ref[0] = (num / (w0 + w1 + w2)).astype(BF16)


def _swa_prompt(slabs, batch, seq):
    in_specs, args = [], []
    for (_, dil), arr in zip(SWA_GROUPS, slabs):
        for qkv in range(3):
            if dil == 1:
                in_specs.append(pl.BlockSpec((1, seq, HEAD_DIM),
                                             lambda b, h, qkv=qkv: (qkv * SWA_HEADS + h, b, 0)))
            else:
                in_specs.append(pl.BlockSpec((1, 1, dil, seq // dil, HEAD_DIM),
                                             lambda b, h, qkv=qkv: (qkv * SWA_HEADS + h, b, 0, 0, 0)))
            args.append(arr)
    scratch = [((N_SWA, seq, HEAD_DIM), F32)] * 2
    return pl.pallas_call(
        functools.partial(_swa_prompt_body, seq=seq),
        grid=(batch, SWA_HEADS),
        in_specs=in_specs,
        out_specs=pl.BlockSpec((1, seq, HEAD_DIM), lambda b, h: (b, 0, h)),
        out_shape=jax.ShapeDtypeStruct((batch, seq, BRANCH_W), BF16),
        scratch_shapes=[pltpu.VMEM(s, d) for s, d in scratch],
        compiler_params=_params(
            ("parallel", "parallel"),
            pipelined=[((1, seq, HEAD_DIM), BF16)] * 10,
            scratch=scratch),
        name="swa_prompt",
    )(*args)


GLA_GROUP_ROWS = MXU_DIM
SAMPLE_SEQS_PER_STEP = 4


def _gla_body(*refs, seq, chunk, has_state, bb):
    if has_state:
        q_ref, k_ref, la_ref, v_ref, r_ref, gn_ref, s0_ref, o_ref, s_ref = refs[:9]
    else:
        q_ref, k_ref, la_ref, v_ref, r_ref, gn_ref, o_ref, s_ref = refs[:8]
    st_ref, qe_ref, ke_ref, kd_ref, vb_ref, dec_ref, oacc_ref = refs[-7:]
    cp = max(chunk, SUBLANES)
    n_chunks = seq // chunk
    assert cp == chunk or n_chunks == 1
    tp = n_chunks * cp
    grp = min(tp, GLA_GROUP_ROWS)
    H, DK, DV = GLA_HEADS, GLA_DK, GLA_DV
    hi = lax.Precision.HIGHEST

    def block_id(shape, axis, size):
        return lax.shift_right_logical(lax.broadcasted_iota(jnp.int32, shape, axis), int(math.log2(size)))

    tri = (lax.broadcasted_iota(jnp.int32, (cp, cp), 0) >=
           lax.broadcasted_iota(jnp.int32, (cp, cp), 1)).astype(F32)
    ai = lax.broadcasted_iota(jnp.int32, (cp, H * cp), 0)
    aj = lax.broadcasted_iota(jnp.int32, (cp, H * cp), 1) & (cp - 1)
    causal = aj <= ai
    k_diag = block_id((H * cp, H * DK), 0, cp) == block_id((H * cp, H * DK), 1, DK)
    v_diag = block_id((H * cp, H * DV), 0, cp) == block_id((H * cp, H * DV), 1, DV)
    s_diag = block_id((H * DV, H * DK), 0, DV) == block_id((H * DV, H * DK), 1, DK)
    gn = gn_ref[...]

    eye_v = (lax.broadcasted_iota(jnp.int32, (DV, DV), 0) ==
             lax.broadcasted_iota(jnp.int32, (DV, DV), 1)).astype(F32)
    eye_k = eye_v[:DK, :DK]

    def pad(x):
        if cp == chunk:
            return x
        return jnp.concatenate([x, jnp.zeros((cp - chunk, x.shape[1]), x.dtype)], axis=0)

    for i in range(bb):
        st_ref[i] = jnp.zeros(st_ref.shape[1:], F32)
        if has_state:
            for h in range(H):
                st_ref[i, h * DV:(h + 1) * DV, h * DK:(h + 1) * DK] = lax.dot_general(
                    eye_v, s0_ref[i, h], _NT, preferred_element_type=F32, precision=hi)

        def chunk_rows(c):
            r0 = c * cp
            return pl.ds(r0 if isinstance(r0, int) else pl.multiple_of(r0, cp), cp)

        def decays(c, carry, i=i):
            rows = chunk_rows(c)
            src = rows if cp == chunk else slice(0, chunk)
            la = pad(la_ref[i, src, :])
            k = pad(k_ref[i, src, :])
            b = jnp.dot(tri, la, preferred_element_type=F32, precision=hi)
            b_end = b[cp - 1:cp, :]
            qe_ref[i, rows, :] = (pad(q_ref[i, src, :]) * jnp.exp(b)).astype(BF16)
            ke_ref[i, rows, :] = (k * jnp.exp(-b)).astype(BF16)
            kd_ref[i, rows, :] = (k * jnp.exp(b_end - b)).astype(BF16)
            vb_ref[i, rows, :] = pad(v_ref[i, src, :]).astype(BF16)
            dec_ref[i, pl.ds(c, 1), :] = jnp.exp(b_end)
            return carry

        def intra(c, carry, i=i):
            rows = chunk_rows(c)
            k_bd = jnp.where(k_diag, jnp.concatenate([ke_ref[i, rows, :]] * H, axis=0), jnp.zeros((), BF16))
            att = lax.dot_general(qe_ref[i, rows, :], k_bd, _NT, preferred_element_type=F32)
            att = jnp.where(causal, att, 0.0).astype(BF16)
            v_bd = jnp.where(v_diag, jnp.concatenate([vb_ref[i, rows, :]] * H, axis=0), jnp.zeros((), BF16))
            oacc_ref[i, rows, :] = _dot(att, v_bd)
            return carry

        def inter(c, carry, i=i):
            rows = chunk_rows(c)
            st = st_ref[i]
            oacc_ref[i, rows, :] += lax.dot_general(qe_ref[i, rows, :], st.astype(BF16), _NT,
                                                    preferred_element_type=F32)
            upd = lax.dot_general(vb_ref[i, rows, :], kd_ref[i, rows, :], _TN,
                                  preferred_element_type=F32)
            st_ref[i] = st * dec_ref[i, pl.ds(c, 1), :] + jnp.where(s_diag, upd, 0.0)
            return carry

        if n_chunks == 1:
            decays(0, 0)
            intra(0, 0)
            inter(0, 0)
        else:
            lax.fori_loop(0, n_chunks, decays, 0, unroll=4 if n_chunks % 4 == 0 else 1)
            lax.fori_loop(0, n_chunks, intra, 0, unroll=4 if n_chunks % 4 == 0 else 1)
            lax.fori_loop(0, n_chunks, inter, 0, unroll=2 if n_chunks % 2 == 0 else 1)

        for gi in range(tp // grp):
            rows = slice(gi * grp, (gi + 1) * grp) if cp == chunk else slice(0, chunk)
            o = oacc_ref[i, rows, :]
            gr = r_ref[i, rows, :]
            for h in range(H):
                cols = slice(h * DV, (h + 1) * DV)
                gh = gr[:, cols]
                o_ref[i, rows, cols] = (_rms(o[:, cols], gn) * (gh * jax.nn.sigmoid(gh))).astype(BF16)

        for h in range(H):
            s_ref[i, h] = lax.dot_general(
                eye_k, st_ref[i, h * DV:(h + 1) * DV, h * DK:(h + 1) * DK], _NT,
                preferred_element_type=F32, precision=hi)


def _gla(gq, gk, la, gv, gr, gn, s0, batch, seq, bb):
    chunk = GLA_CHUNK if seq % GLA_CHUNK == 0 else seq
    tp = (seq // chunk) * max(chunk, SUBLANES)
    has_state = s0 is not None
    row = lambda w: pl.BlockSpec((bb, seq, w), lambda b: (b, 0, 0))
    st_spec = pl.BlockSpec((bb, GLA_HEADS, GLA_DK, GLA_DV), lambda b: (b, 0, 0, 0))
    args = [gq, gk, la, gv, gr, gn] + ([s0] if has_state else [])
    in_specs = [row(GLA_QW)] * 3 + [row(GLA_VW)] * 2 + [_resident(gn.shape)] + ([st_spec] if has_state else [])
    scratch = [((bb, GLA_VW, GLA_QW), F32)] + [((bb, tp, GLA_QW), BF16)] * 3 + [
        ((bb, tp, GLA_VW), BF16), ((bb, max(seq // chunk, SUBLANES), GLA_QW), F32), ((bb, tp, GLA_VW), F32)]
    return pl.pallas_call(
        functools.partial(_gla_body, seq=seq, chunk=chunk, has_state=has_state, bb=bb),
        grid=(batch // bb,),
        in_specs=in_specs,
        out_specs=[row(GLA_VW), st_spec],
        out_shape=[jax.ShapeDtypeStruct((batch, seq, GLA_VW), BF16),
                   jax.ShapeDtypeStruct((batch, GLA_HEADS, GLA_DK, GLA_DV), F32)],
        scratch_shapes=[pltpu.VMEM(s, d) for s, d in scratch],
        compiler_params=_params(
            ("parallel",),
            pipelined=[((bb, seq, GLA_QW), F32)] * 3 + [((bb, seq, GLA_VW), F32)] * 2
            + [((bb, seq, GLA_VW), BF16)] + [((bb, GLA_HEADS, GLA_DK, GLA_DV), F32)] * 2,
            scratch=scratch),
        name="gla_state" if has_state else "gla",
    )(*args)


MEM_Q_CHUNK = 512


def _mem_prompt_body(mem_ref, g_ref, w_ref, q_ref, kv_ref, o_ref, *, seq):
    kv = _dot(_rms(mem_ref[0], g_ref[...]).astype(BF16), w_ref[...])
    for c in range(KV_ROWS):
        kv_ref[0, pl.ds(c, MEM_LEN, stride=KV_ROWS), :] = kv[:, c * HEAD_DIM:(c + 1) * HEAD_DIM]
    hw = MEM_HEADS * HEAD_DIM
    for h in range(MEM_HEADS):
        k = kv[:, h * HEAD_DIM:(h + 1) * HEAD_DIM].astype(BF16)
        v = kv[:, hw + h * HEAD_DIM:hw + (h + 1) * HEAD_DIM].astype(BF16)
        for c in range(seq // MEM_Q_CHUNK):
            rows = slice(c * MEM_Q_CHUNK, (c + 1) * MEM_Q_CHUNK)
            q = q_ref[0, rows, h * HEAD_DIM:(h + 1) * HEAD_DIM]
            s = lax.dot_general(q, k, _NT, preferred_element_type=F32)
            p = jnp.exp(s - jnp.max(s, axis=-1, keepdims=True))
            l = jnp.sum(p, axis=-1, keepdims=True)
            o_ref[0, rows, h * HEAD_DIM:(h + 1) * HEAD_DIM] = (_dot(p.astype(BF16), v) / l).astype(BF16)


def _mem_prompt(mem, g, w, mq, batch, seq):
    kvw = 2 * MEM_HEADS * HEAD_DIM
    return pl.pallas_call(
        functools.partial(_mem_prompt_body, seq=seq),
        grid=(batch,),
        in_specs=[pl.BlockSpec((1, MEM_LEN, D_MODEL), lambda b: (b, 0, 0)), _resident(g.shape),
                  _resident(w.shape), pl.BlockSpec((1, seq, BRANCH_W), lambda b: (b, 0, 0))],
        out_specs=[pl.BlockSpec((1, MEM_LEN * KV_ROWS, LANES), lambda b: (b, 0, 0)),
                   pl.BlockSpec((1, seq, BRANCH_W), lambda b: (b, 0, 0))],
        out_shape=[jax.ShapeDtypeStruct((batch, MEM_LEN * KV_ROWS, LANES), F32),
                   jax.ShapeDtypeStruct((batch, seq, BRANCH_W), BF16)],
        compiler_params=_params(
            ("parallel",),
            pipelined=[((MEM_LEN, D_MODEL), F32), ((seq, BRANCH_W), BF16), ((MEM_LEN, kvw), F32),
                       ((seq, BRANCH_W), BF16)],
            resident=[(g.shape, g.dtype), (w.shape, w.dtype)]),
        name="mem_prompt",
    )(mem, g, w, mq)


def _sample_attn_body(q_ref, xn_ref, c0_ref, c1_ref, c2_ref, cm_ref, b0_ref, b1_ref, b2_ref, bm_ref,
                      bn_ref, oswa_ref, omem_ref):
    half = SWA_HEADS

    def kv_swapped(x):
        x3 = x.reshape(x.shape[0] // SUBLANES, SUBLANES, LANES)
        return pltpu.roll(x3, half, axis=1).reshape(x.shape)

    def attend(q, x, bias, xn=None, bias_n=None):
        xk = x.astype(BF16)
        xv = kv_swapped(x).astype(BF16)
        s = lax.dot_general(q, xk, _NT, preferred_element_type=F32) + bias
        m = jnp.max(s, axis=-1, keepdims=True)
        if xn is not None:
            nk = xn.astype(BF16)
            nv = kv_swapped(xn).astype(BF16)
            sn = lax.dot_general(q, nk, _NT, preferred_element_type=F32) + bias_n
            m = jnp.maximum(m, jnp.max(sn, axis=-1, keepdims=True))
        p = jnp.exp(s - m)
        l = jnp.sum(p, axis=-1, keepdims=True)
        o = _dot(p.astype(BF16), xv)
        if xn is not None:
            pn = jnp.exp(sn - m)
            l = l + jnp.sum(pn, axis=-1, keepdims=True)
            o = o + _dot(pn.astype(BF16), nv)
        return o / l, m + jnp.log(l)

    caches = (c0_ref[0], c1_ref[0].reshape(-1, LANES), c2_ref[0].reshape(-1, LANES))
    biases = (b0_ref, b1_ref, b2_ref)
    outs, lses = [], []
    for g in range(N_SWA):
        o, lse = attend(q_ref[0, g], caches[g], biases[g][...], xn_ref[0, g], bn_ref[g])
        outs.append(o)
        lses.append(lse)
    mx = jnp.maximum(jnp.maximum(lses[0], lses[1]), lses[2])
    ws = [jnp.exp(l - mx) for l in lses]
    num = ws[0] * outs[0] + ws[1] * outs[1] + ws[2] * outs[2]
    oswa_ref[0] = (num / (ws[0] + ws[1] + ws[2])).astype(BF16)
    om, _ = attend(q_ref[0, N_SWA], cm_ref[0], bm_ref[...])
    omem_ref[0] = om.astype(BF16)


def _sample_bias_tables(dec_seq):
    rows_q = dec_seq * SWA_HEADS
    t = (np.arange(rows_q) // SWA_HEADS)[:, None]
    hq = (np.arange(rows_q) % SWA_HEADS)[:, None]

    def table(ok):
        return np.where(ok, 0.0, NEG).astype(np.float32)

    cache_tabs, new_tabs = [], []
    for win, dil in SWA_GROUPS:
        length = min(win, PAST_LEN)
        res = min(dil, dec_seq)
        r = np.arange((length // dil) * res * 2 * SWA_HEADS)[None, :]
        m, rr, kv, hl = r // (res * 8), (r // 8) % res, (r // 4) % 2, r % 4
        delta = length + t - (m * dil + rr)
        ok = (kv == 0) & (hl == hq) & (delta % dil == 0) & (delta // dil >= 0) & (delta // dil <= SWA_BAND)
        cache_tabs.append(table(ok))
        rn = np.arange(dec_seq * 2 * SWA_HEADS)[None, :]
        tn, kvn, hn = rn // 8, (rn // 4) % 2, rn % 4
        dn = t - tn
        okn = (kvn == 0) & (hn == hq) & (dn % dil == 0) & (dn // dil >= 0) & (dn // dil <= SWA_BAND)
        new_tabs.append(table(okn))
    rm = np.arange(MEM_LEN * 2 * MEM_HEADS)[None, :]
    mem_tab = table(((rm // 4) % 2 == 0) & (rm % 4 == hq))
    return cache_tabs, np.stack(new_tabs), mem_tab


def _sample_attn(qall, xnew, cache_swa, cache_mem, dec_batch, dec_seq):
    rows_q = dec_seq * SWA_HEADS
    cache_tabs, new_tab, mem_tab = _sample_bias_tables(dec_seq)
    kvh = 2 * SWA_HEADS
    views, specs, blocks = [], [], []
    for (win, dil), c in zip(SWA_GROUPS, cache_swa):
        length = min(win, PAST_LEN)
        res = min(dil, dec_seq)
        if dil == 1:
            v = c.reshape(dec_batch, length * kvh, LANES)
            blk = (1, length * kvh, LANES)
            specs.append(pl.BlockSpec(blk, lambda b: (b, 0, 0)))
        else:
            v = c.reshape(dec_batch, length // dil, dil * kvh, LANES)
            blk = (1, length // dil, res * kvh, LANES)
            specs.append(pl.BlockSpec(blk, lambda b: (b, 0, 0, 0)))
        views.append(v)
        blocks.append((blk, F32))
    cm = cache_mem.reshape(dec_batch, MEM_LEN * kvh, LANES)
    cm_blk = (1, MEM_LEN * kvh, LANES)
    consts = [jnp.asarray(tb) for tb in cache_tabs] + [jnp.asarray(mem_tab), jnp.asarray(new_tab)]
    out_blk = pl.BlockSpec((1, rows_q, HEAD_DIM), lambda b: (b, 0, 0))
    return pl.pallas_call(
        _sample_attn_body,
        grid=(dec_batch,),
        in_specs=[pl.BlockSpec((1, N_SWA + 1, rows_q, HEAD_DIM), lambda b: (b, 0, 0, 0)),
                  pl.BlockSpec((1, N_SWA, dec_seq * kvh, HEAD_DIM), lambda b: (b, 0, 0, 0))]
        + specs + [pl.BlockSpec(cm_blk, lambda b: (b, 0, 0))] + [_resident(c.shape) for c in consts],
        out_specs=[out_blk, out_blk],
        out_shape=[jax.ShapeDtypeStruct((dec_batch, rows_q, HEAD_DIM), BF16)] * 2,
        compiler_params=_params(
            ("parallel",),
            pipelined=blocks + [(cm_blk, F32)],
            resident=[(c.shape, c.dtype) for c in consts]),
        name="sample_attn",
    )(qall, xnew, *views, cm, *consts)


def _merge_body(h_ref, gate_ref, oa_ref, ob_ref, oc_ref, wbr_ref, wo_ref, o_ref):
    mix = None
    for n, o_b in enumerate((oa_ref, ob_ref, oc_ref)):
        br = _dot(o_b[...], wbr_ref[n])
        term = gate_ref[:, n * D_MODEL:(n + 1) * D_MODEL].astype(F32) * br
        mix = term if mix is None else mix + term
    o_ref[...] = h_ref[...] + _dot(mix.astype(BF16), wo_ref[...])


def _merge(h, gates, o_swa, o_gla, o_mem, wbr, wo):
    n = h.shape[0]
    tm = min(TOKEN_TILE, n)
    row = lambda w: pl.BlockSpec((tm, w), lambda i: (i, 0))
    return pl.pallas_call(
        _merge_body,
        grid=(n // tm,),
        in_specs=[row(D_MODEL), row(N_BRANCH * D_MODEL), row(BRANCH_W), row(BRANCH_W), row(BRANCH_W),
                  _resident(wbr.shape), _resident(wo.shape)],
        out_specs=row(D_MODEL),
        out_shape=jax.ShapeDtypeStruct((n, D_MODEL), F32),
        compiler_params=_params(
            ("parallel",),
            pipelined=[((tm, D_MODEL), F32)] * 2 + [((tm, N_BRANCH * D_MODEL), BF16)]
            + [((tm, BRANCH_W), BF16)] * 3,
            resident=[(wbr.shape, wbr.dtype), (wo.shape, wo.dtype)]),
        name="merge",
    )(h, gates, o_swa, o_gla, o_mem, wbr, wo)


def _rope_tables(positions):
    half = HEAD_DIM // 2
    inv = ROPE_THETA ** (-np.arange(half, dtype=np.float64) / half)
    ang = np.asarray(positions, np.float64)[:, None] * inv[None, :]
    cos, sin = np.cos(ang), np.sin(ang)
    return (jnp.asarray(np.concatenate([cos, cos], axis=1), F32),
            jnp.asarray(np.concatenate([-sin, sin], axis=1), F32))


def kernel(x_prompt, x_sample, mem_prompt, cache_swa0, cache_swa1, cache_swa2, cache_mem_kv, state_gla,
           g_ffn1, w_ffn1_in, w_ffn1_out, g_mix, w_in, w_gla_a2, b_gla_a, g_gla_out, w_gate, w_branch,
           w_out, g_mem, w_mem_kv, g_ffn2, w_ffn2_in, w_ffn2_out, g_final):
    batch, seq, _ = x_prompt.shape
    dec_batch, dec_seq, _ = x_sample.shape
    depth = g_ffn1.shape[0]
    assert depth == 1, "single-layer step"
    l = 0
    row = lambda g: g.reshape(1, -1)

    w1i, w2i = w_ffn1_in[l].astype(BF16), w_ffn2_in[l].astype(BF16)
    w1a, w1b, w1o = w1i[:, :D_FF], w1i[:, D_FF:], w_ffn1_out[l].astype(BF16)
    w2a, w2b, w2o = w2i[:, :D_FF], w2i[:, D_FF:], w_ffn2_out[l].astype(BF16)
    wi = w_in[l].astype(BF16)
    o_gla0 = 3 * SWA_W
    o_ga = o_gla0 + 2 * GLA_QW + 2 * GLA_VW
    w_swa = wi[:, :o_gla0]
    w_g = wi[:, o_gla0:o_ga]
    w_ga = jnp.pad(wi[:, o_ga:o_ga + GLA_RANK], ((0, 0), (0, LANES - GLA_RANK)))
    w_a2 = jnp.pad(w_gla_a2[l].astype(BF16), ((0, LANES - GLA_RANK), (0, 0)))
    w_mq = wi[:, o_ga + GLA_RANK:]
    wgate = w_gate[l].astype(BF16)
    wbr = w_branch[l].astype(BF16)
    wo = w_out[l].astype(BF16)
    wmem = w_mem_kv[l].astype(BF16)
    weights_misc = (row(g_mix[l]), w_g, w_ga, w_a2, row(b_gla_a[l]), w_mq, wgate)

    n_p = batch * seq
    cos_p, sin_p = _rope_tables(np.arange(seq))
    hp = _ffn(x_prompt.reshape(n_p, D_MODEL), row(g_ffn1[l]), w1a, w1b, w1o)
    *slabs, rows0, rows1, rows2 = _proj_swa_prompt(hp, row(g_mix[l]), cos_p, sin_p, w_swa, batch, seq)
    gq, gk, gv, gr, la, mq, gates = _proj_misc(hp, *weights_misc)
    o_swa = _swa_prompt(slabs, batch, seq)
    r3 = lambda a: a.reshape(batch, seq, a.shape[-1])
    o_gla, gla_p = _gla(r3(gq), r3(gk), r3(la), r3(gv), r3(gr), row(g_gla_out[l]), None, batch, seq, 1)
    kv_p, o_mem = _mem_prompt(mem_prompt, row(g_mem[l]), wmem, r3(mq), batch, seq)
    hp = _merge(hp, gates, o_swa.reshape(n_p, BRANCH_W), o_gla.reshape(n_p, BRANCH_W),
                o_mem.reshape(n_p, BRANCH_W), wbr, wo)
    y_prompt = _ffn(hp, row(g_ffn2[l]), w2a, w2b, w2o, row(g_final)).reshape(batch, seq, D_MODEL)

    swa_p = [r.reshape(1, batch, min(win, seq), 2, SWA_HEADS, HEAD_DIM)
             for r, (win, _) in zip((rows0, rows1, rows2), SWA_GROUPS)]
    mem_kv_p = kv_p.reshape(1, batch, MEM_LEN, 2, MEM_HEADS, HEAD_DIM)

    n_s = dec_batch * dec_seq
    cos_s, sin_s = _rope_tables(PAST_LEN + (np.arange(n_s) % dec_seq))
    hs = _ffn(x_sample.reshape(n_s, D_MODEL), row(g_ffn1[l]), w1a, w1b, w1o)
    slabs_s = _proj_swa_sample(hs, row(g_mix[l]), cos_s, sin_s, w_swa)
    gq, gk, gv, gr, la, mq, gates = _proj_misc(hs, *weights_misc)
    s3 = lambda a: a.reshape(dec_batch, dec_seq, a.shape[-1])
    o_gla_s, gla_s = _gla(s3(gq), s3(gk), s3(la), s3(gv), s3(gr), row(g_gla_out[l]), state_gla[l],
                          dec_batch, dec_seq, SAMPLE_SEQS_PER_STEP)
    s6 = slabs_s.reshape(3, N_SWA, SWA_HEADS, dec_batch, dec_seq, HEAD_DIM)
    q_swa = jnp.transpose(s6[0], (2, 0, 3, 1, 4)).reshape(dec_batch, N_SWA, dec_seq * SWA_HEADS, HEAD_DIM)
    q_mem = mq.reshape(dec_batch, 1, dec_seq * MEM_HEADS, HEAD_DIM)
    qall = jnp.concatenate([q_swa.astype(BF16), q_mem], axis=1)
    xnew = jnp.transpose(s6[1:], (3, 1, 4, 0, 2, 5))
    o_swa_s, o_mem_s = _sample_attn(
        qall, xnew.reshape(dec_batch, N_SWA, dec_seq * 2 * SWA_HEADS, HEAD_DIM),
        (cache_swa0[l], cache_swa1[l], cache_swa2[l]), cache_mem_kv[l], dec_batch, dec_seq)
    hs = _merge(hs, gates, o_swa_s.reshape(n_s, BRANCH_W), o_gla_s.reshape(n_s, BRANCH_W),
                o_mem_s.reshape(n_s, BRANCH_W), wbr, wo)
    y_sample = _ffn(hs, row(g_ffn2[l]), w2a, w2b, w2o, row(g_final)).reshape(dec_batch, dec_seq, D_MODEL)
    swa_s = [xnew[:, g][None] for g in range(N_SWA)]

    return (y_prompt, y_sample, swa_p[0], swa_p[1], swa_p[2], mem_kv_p, gla_p[None],
            swa_s[0], swa_s[1], swa_s[2], gla_s[None])
```

```python
import functools
import math

import numpy as np
import jax
import jax.numpy as jnp
from jax import lax
from jax.experimental import pallas as pl
from jax.experimental.pallas import tpu as pltpu

F32 = jnp.float32
BF16 = jnp.bfloat16

D_MODEL = 1024
PAST_LEN = 8192
HEAD_DIM = 128
SWA_GROUPS = ((128, 1), (512, 4), (2048, 16))
N_SWA = 3
SWA_HEADS = 4
SWA_BAND = 128
GLA_HEADS = 4
GLA_DK = 64
GLA_DV = 128
GLA_RANK = 16
GLA_TAU = 16.0
GLA_CHUNK = 64
MEM_LEN = 256
MEM_HEADS = 4
BRANCH_W = 512
N_BRANCH = 3
D_FF = 2816
ROPE_THETA = 10000.0
EPS = 1e-6
NEG = -1e30

SWA_W = N_SWA * SWA_HEADS * HEAD_DIM
GLA_QW = GLA_HEADS * GLA_DK
GLA_VW = GLA_HEADS * GLA_DV

LANES = 128
SUBLANES = 8
MXU_DIM = 256
VMEM_BYTES_V7X = 64 * 1024 * 1024
VMEM_INTERNAL_RESERVE = 10 * 1024 * 1024

TOKEN_TILE = 512
FF_CHUNK = MXU_DIM

_NT = (((1,), (1,)), ((), ()))
_TN = (((0,), (0,)), ((), ()))


def _nbytes(shape, dtype):
    return math.prod(shape) * jnp.dtype(dtype).itemsize


def _params(semantics, pipelined, resident=(), scratch=()):
    need = 2 * sum(_nbytes(s, d) for s, d in pipelined)
    need += sum(_nbytes(s, d) for s, d in resident)
    need += sum(_nbytes(s, d) for s, d in scratch)
    limit = min(need + VMEM_INTERNAL_RESERVE, VMEM_BYTES_V7X - 4 * 1024 * 1024)
    return pltpu.CompilerParams(dimension_semantics=semantics, vmem_limit_bytes=limit)


def _resident(shape):
    del shape
    return pl.BlockSpec(memory_space=pltpu.VMEM)


def _rms(x, g):
    return x * lax.rsqrt(jnp.mean(x * x, axis=-1, keepdims=True) + EPS) * g


def _dot(a, b):
    return jnp.dot(a, b, preferred_element_type=F32)


def _half_step(x, g_ref, wi_ref, wo_ref, hid_ref):
    xb = _rms(x, g_ref[...]).astype(BF16)
    for c in range(D_FF // FF_CHUNK):
        sl = slice(c * FF_CHUNK, (c + 1) * FF_CHUNK)
        a = _dot(xb, wi_ref[:, sl])
        b = _dot(xb, wi_ref[:, D_FF + c * FF_CHUNK:D_FF + (c + 1) * FF_CHUNK])
        hid_ref[:, sl] = (a * jax.nn.sigmoid(a) * b).astype(BF16)
    return x + 0.5 * _dot(hid_ref[...], wo_ref[...])


def _ffn_body(x_ref, g_ref, wi_ref, wo_ref, o_ref, hid_ref):
    o_ref[...] = _half_step(x_ref[...], g_ref, wi_ref, wo_ref, hid_ref)


def _ffn(x, g, wi, wo):
    n = x.shape[0]
    tm = min(TOKEN_TILE, n)
    tile = pl.BlockSpec((tm, D_MODEL), lambda i: (i, 0))
    weights = [g, wi, wo]
    return pl.pallas_call(
        _ffn_body,
        grid=(n // tm,),
        in_specs=[tile] + [_resident(a.shape) for a in weights],
        out_specs=tile,
        out_shape=jax.ShapeDtypeStruct((n, D_MODEL), F32),
        scratch_shapes=[pltpu.VMEM((tm, D_FF), BF16)],
        compiler_params=_params(
            ("parallel",),
            pipelined=[((tm, D_MODEL), F32)] * 2,
            resident=[(a.shape, a.dtype) for a in weights],
            scratch=[((tm, D_FF), BF16)]),
        name="ffn",
    )(x, *weights)


KV_ROWS = 2 * SWA_HEADS
REGROUP_STRIDE = 4
QKV_SLABS = 3 * SWA_HEADS


def _proj_swa_body(*refs, seq, tm, prompt):
    if prompt:
        h_ref, g_ref, cos_ref, sin_ref, w_ref = refs[:5]
        slab_refs, row_refs = refs[5:8], refs[8:11]
        stage_ref, tail_ref, regroup_ref = refs[11:14]
        last_tile = pl.program_id(0) % (seq // tm) == seq // tm - 1
    else:
        h_ref, g_ref, cos_ref, sin_ref, w_ref, o_ref = refs
    u = _rms(h_ref[...], g_ref[...]).astype(BF16)
    cos = cos_ref[...]
    sin = sin_ref[...]
    n_chunk = 3 * N_SWA
    z_next = _dot(u, w_ref[:, :BRANCH_W])
    for c in range(n_chunk):
        qkv, g = divmod(c, N_SWA)
        win, dil = SWA_GROUPS[g]
        keep = min(win, seq)
        z = z_next
        if c + 1 < n_chunk:
            z_next = _dot(u, w_ref[:, (c + 1) * BRANCH_W:(c + 2) * BRANCH_W])
        for j in range(SWA_HEADS):
            s = z[:, j * HEAD_DIM:(j + 1) * HEAD_DIM]
            if qkv < 2:
                s = s * cos + pltpu.roll(s, HEAD_DIM // 2, axis=1) * sin
            if qkv == 0:
                s = s * (HEAD_DIM ** -0.5)
            slab = qkv * SWA_HEADS + j
            if not prompt:
                o_ref[c * SWA_HEADS + j] = s
                continue
            if dil == 1:
                slab_refs[g][slab] = s.astype(BF16)
                if qkv > 0 and keep < seq:
                    tail_ref[slab - SWA_HEADS] = s[tm - keep:, :]
            else:
                k = (g - 1) * QKV_SLABS + slab
                stage_ref[k] = s
                if dil % (REGROUP_STRIDE * REGROUP_STRIDE) == 0:
                    coarse = dil // REGROUP_STRIDE
                    for cc in range(REGROUP_STRIDE):
                        regroup_ref[slab, cc] = stage_ref[k, pl.ds(cc, tm // REGROUP_STRIDE, stride=REGROUP_STRIDE), :]
                    for r in range(dil):
                        a, cc = divmod(r, REGROUP_STRIDE)
                        slab_refs[g][slab, 0, r] = regroup_ref[
                            slab, cc, pl.ds(a, tm // dil, stride=coarse), :].astype(BF16)
                else:
                    for r in range(dil):
                        slab_refs[g][slab, 0, r] = stage_ref[k, pl.ds(r, tm // dil, stride=dil), :].astype(BF16)
            if qkv > 0 and keep == seq:
                row_refs[g][pl.ds(slab - SWA_HEADS, tm, stride=KV_ROWS), :] = s
    if prompt:
        @pl.when(last_tile)
        def _():
            for g, (win, dil) in enumerate(SWA_GROUPS):
                keep = min(win, seq)
                if keep == seq:
                    continue
                for c_row in range(KV_ROWS):
                    if dil == 1:
                        src = tail_ref[c_row]
                    else:
                        src = stage_ref[(g - 1) * QKV_SLABS + SWA_HEADS + c_row, tm - keep:, :]
                    row_refs[g][pl.ds(c_row, keep, stride=KV_ROWS), :] = src


def _proj_swa_sample(h, g, cos, sin, w):
    n = h.shape[0]
    n_slabs = N_SWA * QKV_SLABS
    return pl.pallas_call(
        functools.partial(_proj_swa_body, seq=n, tm=n, prompt=False),
        grid=(1,),
        in_specs=[pl.BlockSpec((n, D_MODEL), lambda i: (0, 0)), _resident(g.shape),
                  pl.BlockSpec((n, HEAD_DIM), lambda i: (0, 0)), pl.BlockSpec((n, HEAD_DIM), lambda i: (0, 0)),
                  _resident(w.shape)],
        out_specs=pl.BlockSpec((n_slabs, n, HEAD_DIM), lambda i: (0, 0, 0)),
        out_shape=jax.ShapeDtypeStruct((n_slabs, n, HEAD_DIM), F32),
        compiler_params=_params(
            ("arbitrary",),
            pipelined=[((n, D_MODEL), F32), ((n, HEAD_DIM), F32), ((n, HEAD_DIM), F32),
                       ((n_slabs, n, HEAD_DIM), F32)],
            resident=[(g.shape, g.dtype), (w.shape, w.dtype)]),
        name="proj_swa_sample",
    )(h, g, cos, sin, w)


def _proj_swa_prompt(h, g, cos, sin, w, batch, seq):
    n = batch * seq
    tm = TOKEN_TILE
    tps = seq // tm
    keeps = [min(win, seq) for win, _ in SWA_GROUPS]
    assert seq % tm == 0 and all(k == seq or k <= tm for k in keeps)
    assert all(tm % (dil * 2 * SUBLANES) == 0 for _, dil in SWA_GROUPS)
    tile = pl.BlockSpec((tm, D_MODEL), lambda i: (i, 0))
    pos = pl.BlockSpec((tm, HEAD_DIM), lambda i: (i % tps, 0))
    out_specs, out_shape, pipelined = [], [], []
    for _, dil in SWA_GROUPS:
        if dil == 1:
            shp, blk = (QKV_SLABS, n, HEAD_DIM), (QKV_SLABS, tm, HEAD_DIM)
            out_specs.append(pl.BlockSpec(blk, lambda i: (0, i, 0)))
        else:
            shp = (QKV_SLABS, batch, dil, seq // dil, HEAD_DIM)
            blk = (QKV_SLABS, 1, dil, tm // dil, HEAD_DIM)
            out_specs.append(pl.BlockSpec(blk, lambda i: (0, i // tps, 0, i % tps, 0)))
        out_shape.append(jax.ShapeDtypeStruct(shp, BF16))
        pipelined.append((blk, BF16))
    for keep in keeps:
        if keep == seq:
            blk = (tm * KV_ROWS, LANES)
            out_specs.append(pl.BlockSpec(blk, lambda i: (i, 0)))
        else:
            blk = (keep * KV_ROWS, LANES)
            out_specs.append(pl.BlockSpec(blk, lambda i: (i // tps, 0)))
        out_shape.append(jax.ShapeDtypeStruct((batch * keep * KV_ROWS, LANES), F32))
        pipelined.append((blk, F32))
    n_dilated = sum(1 for _, dil in SWA_GROUPS if dil > 1)
    scratch = [((n_dilated * QKV_SLABS, tm, HEAD_DIM), F32), ((KV_ROWS, keeps[0], HEAD_DIM), F32),
               ((QKV_SLABS, REGROUP_STRIDE, tm // REGROUP_STRIDE, HEAD_DIM), F32)]
    return pl.pallas_call(
        functools.partial(_proj_swa_body, seq=seq, tm=tm, prompt=True),
        grid=(n // tm,),
        in_specs=[tile, _resident(g.shape), pos, pos, _resident(w.shape)],
        out_specs=out_specs,
        out_shape=out_shape,
        scratch_shapes=[pltpu.VMEM(s, d) for s, d in scratch],
        compiler_params=_params(
            ("arbitrary",),
            pipelined=[((tm, D_MODEL), F32), ((tm, HEAD_DIM), F32), ((tm, HEAD_DIM), F32)] + pipelined,
            resident=[(g.shape, g.dtype), (w.shape, w.dtype)],
            scratch=scratch),
        name="proj_swa",
    )(h, g, cos, sin, w)


def _log_sigmoid(x):
    return jnp.minimum(x, 0.0) - jnp.log1p(jnp.exp(-jnp.abs(x)))


def _proj_misc_body(h_ref, g_ref, wg_ref, wga_ref, wa2_ref, ba_ref, wmq_ref, wgate_ref,
                    gq_ref, gk_ref, gv_ref, gr_ref, la_ref, mq_ref, gate_ref):
    u = _rms(h_ref[...], g_ref[...]).astype(BF16)
    zg = _dot(u, wg_ref[...])
    gq_ref[...] = zg[:, :GLA_QW] * (GLA_DK ** -0.5)
    gk_ref[...] = zg[:, GLA_QW:2 * GLA_QW]
    gv_ref[...] = zg[:, 2 * GLA_QW:2 * GLA_QW + GLA_VW]
    gr_ref[...] = zg[:, 2 * GLA_QW + GLA_VW:]
    za = _dot(u, wga_ref[...]).astype(BF16)
    la_ref[...] = _log_sigmoid(_dot(za, wa2_ref[...]) + ba_ref[...]) / GLA_TAU
    mq_ref[...] = (_dot(u, wmq_ref[...]) * (HEAD_DIM ** -0.5)).astype(BF16)
    for c in range(N_BRANCH):
        sl = slice(c * D_MODEL, (c + 1) * D_MODEL)
        gate_ref[:, sl] = jax.nn.sigmoid(_dot(u, wgate_ref[:, sl])).astype(BF16)


def _proj_misc(h, g, wg, wga, wa2, ba, wmq, wgate):
    n = h.shape[0]
    tm = min(TOKEN_TILE, n)
    row = lambda w: pl.BlockSpec((tm, w), lambda i: (i, 0))
    outs = [(GLA_QW, F32), (GLA_QW, F32), (GLA_VW, F32), (GLA_VW, F32), (GLA_QW, F32),
            (BRANCH_W, BF16), (N_BRANCH * D_MODEL, BF16)]
    weights = [g, wg, wga, wa2, ba, wmq, wgate]
    return pl.pallas_call(
        _proj_misc_body,
        grid=(n // tm,),
        in_specs=[row(D_MODEL)] + [_resident(w.shape) for w in weights],
        out_specs=[row(w) for w, _ in outs],
        out_shape=[jax.ShapeDtypeStruct((n, w), d) for w, d in outs],
        compiler_params=_params(
            ("parallel",),
            pipelined=[((tm, D_MODEL), F32)] + [((tm, w), d) for w, d in outs],
            resident=[(w.shape, w.dtype) for w in weights]),
        name="proj_misc",
    )(h, *weights)


def _swa_prompt_body(*refs, seq):
    o_ref, oacc_ref, lacc_ref = refs[9:12]
    blk = SWA_BAND
    plane = seq // REGROUP_STRIDE
    assert all(dil == 1 or dil % REGROUP_STRIDE == 0 for _, dil in SWA_GROUPS)
    gap = (lax.broadcasted_iota(jnp.int32, (blk, 2 * blk), 1) -
           lax.broadcasted_iota(jnp.int32, (blk, 2 * blk), 0))
    ok_cat = (gap >= 0) & (gap <= blk)
    ok_own = (lax.broadcasted_iota(jnp.int32, (blk, blk), 1) <=
              lax.broadcasted_iota(jnp.int32, (blk, blk), 0))
    ones = jnp.ones((2 * blk, blk), BF16)

    def attend_all(ops):
        ss = [lax.dot_general(q, kc, _NT, preferred_element_type=F32) for q, kc, _, _ in ops]
        ss = [jnp.where(op[3], s, NEG) for s, op in zip(ss, ops)]
        ms = [jnp.max(s, axis=-1, keepdims=True) for s in ss]
        ps = [jnp.exp(s - m).astype(BF16) for s, m in zip(ss, ms)]
        accs = [_dot(p, jnp.concatenate([op[2], ones[:op[2].shape[0]]], axis=1)) for p, op in zip(ps, ops)]
        return [(acc[:, :blk] / acc[:, blk:], m + jnp.log(acc[:, blk:])) for acc, m in zip(accs, ms)]

    def attend(q, kc, vc, ok):
        return attend_all([(q, kc, vc, ok)])[0]

    def put(g, rows, res):
        oacc_ref[g, rows, :] = res[0]
        lacc_ref[g, rows, :] = res[1]

    for g, (_, dil) in enumerate(SWA_GROUPS):
        q_ref, k_ref, v_ref = refs[3 * g:3 * g + 3]
        n_blk = seq // dil // blk
        ops, rows = [], []
        for r in range(dil):
            for n in range(n_blk):
                keys = slice(max(n - 1, 0) * blk, (n + 1) * blk)
                ok = ok_own if n == 0 else ok_cat
                if dil == 1:
                    ops.append((q_ref[0, n * blk:(n + 1) * blk], k_ref[0, keys], v_ref[0, keys], ok))
                    rows.append(pl.ds(n * blk, blk))
                else:
                    ops.append((q_ref[0, 0, r, n * blk:(n + 1) * blk], k_ref[0, 0, r, keys],
                                v_ref[0, 0, r, keys], ok))
                    a, c = divmod(r, REGROUP_STRIDE)
                    coarse = dil // REGROUP_STRIDE
                    first = c * plane + a + n * blk * coarse
                    rows.append(pl.ds(first, blk) if coarse == 1 else pl.ds(first, blk, stride=coarse))
        for rw, out in zip(rows, attend_all(ops)):
            put(g, rw, out)

    for c in range(REGROUP_STRIDE):
        nat = pl.ds(c, plane, stride=REGROUP_STRIDE)
        pln = pl.ds(c * plane, plane)
        l0, l1, l2 = lacc_ref[0, nat, :], lacc_ref[1, pln, :], lacc_ref[2, pln, :]
        mx = jnp.maximum(jnp.maximum(l0, l1), l2)
        w0, w1, w2 = jnp.exp(l0 - mx), jnp.exp(l1 - mx), jnp.exp(l2 - mx)
        num = w0 * oacc_ref[0, nat, :] + w1 * oacc_ref[1, pln, :] + w2 * oacc_ref[2, pln, :]
        oacc_ref[0, nat, :] = num / (w0 + w1 + w2)
    o_ref[0] = oacc_ref[0].astype(BF16)


def _swa_prompt(slabs, batch, seq):
    in_specs, args = [], []
    for (_, dil), arr in zip(SWA_GROUPS, slabs):
        for qkv in range(3):
            if dil == 1:
                in_specs.append(pl.BlockSpec((1, seq, HEAD_DIM),
                                             lambda b, h, qkv=qkv: (qkv * SWA_HEADS + h, b, 0)))
            else:
                in_specs.append(pl.BlockSpec((1, 1, dil, seq // dil, HEAD_DIM),
                                             lambda b, h, qkv=qkv: (qkv * SWA_HEADS + h, b, 0, 0, 0)))
            args.append(arr)
    scratch = [((N_SWA, seq, HEAD_DIM), F32)] * 2
    return pl.pallas_call(
        functools.partial(_swa_prompt_body, seq=seq),
        grid=(batch, SWA_HEADS),
        in_specs=in_specs,
        out_specs=pl.BlockSpec((1, seq, HEAD_DIM), lambda b, h: (b, 0, h)),
        out_shape=jax.ShapeDtypeStruct((batch, seq, BRANCH_W), BF16),
        scratch_shapes=[pltpu.VMEM(s, d) for s, d in scratch],
        compiler_params=_params(
            ("parallel", "parallel"),
            pipelined=[((1, seq, HEAD_DIM), BF16)] * 10,
            scratch=scratch),
        name="swa_prompt",
    )(*args)


GLA_CHUNKS_PER_TRIP = 8
SAMPLE_SEQS_PER_STEP = 4
SAMPLE_ATTN_SEQS_PER_STEP = 2


def _gla_body(*refs, seq, chunk, has_state, bb):
    if has_state:
        q_ref, k_ref, la_ref, v_ref, r_ref, gn_ref, s0_ref, o_ref, s_ref, st_ref = refs
    else:
        q_ref, k_ref, la_ref, v_ref, r_ref, gn_ref, o_ref, s_ref, st_ref = refs
    cp = max(chunk, SUBLANES)
    n_chunks = seq // chunk
    assert cp == chunk or n_chunks == 1
    per_trip = math.gcd(n_chunks, GLA_CHUNKS_PER_TRIP)
    H, DK, DV = GLA_HEADS, GLA_DK, GLA_DV

    def block_id(shape, axis, size):
        return lax.shift_right_logical(lax.broadcasted_iota(jnp.int32, shape, axis), int(math.log2(size)))

    tri = (lax.broadcasted_iota(jnp.int32, (cp, cp), 0) >=
           lax.broadcasted_iota(jnp.int32, (cp, cp), 1)).astype(BF16)

    def cumsum(x):
        p0 = x.astype(BF16)
        r1 = x - p0.astype(F32)
        p1 = r1.astype(BF16)
        p2 = (r1 - p1.astype(F32)).astype(BF16)
        y = _dot(tri, jnp.concatenate([p0, p1, p2], axis=1))
        w = x.shape[1]
        return y[:, :w] + y[:, w:2 * w] + y[:, 2 * w:]
    ai = lax.broadcasted_iota(jnp.int32, (cp, H * cp), 0)
    aj = lax.broadcasted_iota(jnp.int32, (cp, H * cp), 1) & (cp - 1)
    causal = aj <= ai
    k_diag = block_id((H * cp, H * DK), 0, cp) == block_id((H * cp, H * DK), 1, DK)
    v_diag = block_id((H * cp, H * DV), 0, cp) == block_id((H * cp, H * DV), 1, DV)
    s_diag = block_id((H * DV, H * DK), 0, DV) == block_id((H * DV, H * DK), 1, DK)
    gn = gn_ref[...]

    def pad(x):
        if cp == chunk:
            return x
        return jnp.concatenate([x, jnp.zeros((cp - chunk, x.shape[1]), x.dtype)], axis=0)

    for i in range(bb):
        st_ref[i] = jnp.zeros(st_ref.shape[1:], F32)
        if has_state:
            for h in range(H):
                st_ref[i, h * DV:(h + 1) * DV, h * DK:(h + 1) * DK] = s0_ref[i, h].T

    def run(items):
        def load(ref, i, r0):
            return pad(ref[i, pl.ds(r0, chunk), :])
        las = [load(la_ref, i, r0) for i, r0 in items]
        ks = [load(k_ref, i, r0) for i, r0 in items]
        qs = [load(q_ref, i, r0) for i, r0 in items]
        vbs = [load(v_ref, i, r0).astype(BF16) for i, r0 in items]
        bs = [cumsum(la) for la in las]
        b_ends = [b[cp - 1:cp, :] for b in bs]
        qes = [(q * jnp.exp(b)).astype(BF16) for q, b in zip(qs, bs)]
        kes = [(k * jnp.exp(-b)).astype(BF16) for k, b in zip(ks, bs)]
        kds = [(k * jnp.exp(e - b)).astype(BF16) for k, b, e in zip(ks, bs, b_ends)]
        decays = [jnp.exp(e) for e in b_ends]
        atts = [lax.dot_general(qe, jnp.where(k_diag, jnp.concatenate([ke] * H, axis=0), jnp.zeros((), BF16)),
                                _NT, preferred_element_type=F32) for qe, ke in zip(qes, kes)]
        upds = [lax.dot_general(vb, kd, _TN, preferred_element_type=F32) for vb, kd in zip(vbs, kds)]
        atts = [jnp.where(causal, att, 0.0).astype(BF16) for att in atts]
        outs = [_dot(att, jnp.where(v_diag, jnp.concatenate([vb] * H, axis=0), jnp.zeros((), BF16)))
                for att, vb in zip(atts, vbs)]
        before, live = [], {}
        for (i, _), decay, upd in zip(items, decays, upds):
            st = live[i] if i in live else st_ref[i]
            before.append(st)
            live[i] = st * decay + jnp.where(s_diag, upd, 0.0)
        for i, st in live.items():
            st_ref[i] = st
        outs = [o + lax.dot_general(qe, st.astype(BF16), _NT, preferred_element_type=F32)
                for o, qe, st in zip(outs, qes, before)]
        for (i, r0), o in zip(items, outs):
            rows = pl.ds(r0, chunk)
            gr = r_ref[i, rows, :]
            for h in range(H):
                cols = slice(h * DV, (h + 1) * DV)
                gh = gr[:, cols]
                o_ref[i, rows, cols] = (_rms(o[:chunk, cols], gn) * (gh * jax.nn.sigmoid(gh))).astype(BF16)

    if n_chunks == 1:
        run([(i, 0) for i in range(bb)])
    else:
        for i in range(bb):
            def trip(t, carry, i=i):
                base = pl.multiple_of(t * (per_trip * chunk), per_trip * chunk)
                run([(i, pl.multiple_of(base + j * chunk, chunk)) for j in range(per_trip)])
                return carry
            lax.fori_loop(0, n_chunks // per_trip, trip, 0)

    for i in range(bb):
        for h in range(H):
            s_ref[i, h] = st_ref[i, h * DV:(h + 1) * DV, h * DK:(h + 1) * DK].T


def _gla(gq, gk, la, gv, gr, gn, s0, batch, seq, bb):
    chunk = GLA_CHUNK if seq % GLA_CHUNK == 0 else seq
    has_state = s0 is not None
    row = lambda w: pl.BlockSpec((bb, seq, w), lambda b: (b, 0, 0))
    st_spec = pl.BlockSpec((bb, GLA_HEADS, GLA_DK, GLA_DV), lambda b: (b, 0, 0, 0))
    args = [gq, gk, la, gv, gr, gn] + ([s0] if has_state else [])
    in_specs = [row(GLA_QW)] * 3 + [row(GLA_VW)] * 2 + [_resident(gn.shape)] + ([st_spec] if has_state else [])
    scratch = [((bb, GLA_VW, GLA_QW), F32)]
    return pl.pallas_call(
        functools.partial(_gla_body, seq=seq, chunk=chunk, has_state=has_state, bb=bb),
        grid=(batch // bb,),
        in_specs=in_specs,
        out_specs=[row(GLA_VW), st_spec],
        out_shape=[jax.ShapeDtypeStruct((batch, seq, GLA_VW), BF16),
                   jax.ShapeDtypeStruct((batch, GLA_HEADS, GLA_DK, GLA_DV), F32)],
        scratch_shapes=[pltpu.VMEM(s, d) for s, d in scratch],
        compiler_params=_params(
            ("parallel",),
            pipelined=[((bb, seq, GLA_QW), F32)] * 3 + [((bb, seq, GLA_VW), F32)] * 2
            + [((bb, seq, GLA_VW), BF16)] + [((bb, GLA_HEADS, GLA_DK, GLA_DV), F32)] * 2,
            scratch=scratch),
        name="gla_state" if has_state else "gla",
    )(*args)


MEM_Q_CHUNK = 512


def _mem_prompt_body(mem_ref, g_ref, w_ref, q_ref, kv_ref, o_ref, *, seq):
    kv = _dot(_rms(mem_ref[0], g_ref[...]).astype(BF16), w_ref[...])
    for c in range(KV_ROWS):
        kv_ref[0, pl.ds(c, MEM_LEN, stride=KV_ROWS), :] = kv[:, c * HEAD_DIM:(c + 1) * HEAD_DIM]
    hw = MEM_HEADS * HEAD_DIM
    ones = jnp.ones((MEM_LEN, HEAD_DIM), BF16)
    chunks = [slice(c * MEM_Q_CHUNK, (c + 1) * MEM_Q_CHUNK) for c in range(seq // MEM_Q_CHUNK)]
    for h in range(MEM_HEADS):
        cols = slice(h * HEAD_DIM, (h + 1) * HEAD_DIM)
        k = kv[:, cols].astype(BF16)
        v_ext = jnp.concatenate([kv[:, hw + h * HEAD_DIM:hw + (h + 1) * HEAD_DIM].astype(BF16), ones], axis=1)
        ss = [lax.dot_general(q_ref[0, rows, cols], k, _NT, preferred_element_type=F32) for rows in chunks]
        ps = [jnp.exp(s - jnp.max(s, axis=-1, keepdims=True)).astype(BF16) for s in ss]
        accs = [_dot(p, v_ext) for p in ps]
        for rows, acc in zip(chunks, accs):
            o_ref[0, rows, cols] = (acc[:, :HEAD_DIM] / acc[:, HEAD_DIM:]).astype(BF16)


def _mem_prompt(mem, g, w, mq, batch, seq):
    kvw = 2 * MEM_HEADS * HEAD_DIM
    return pl.pallas_call(
        functools.partial(_mem_prompt_body, seq=seq),
        grid=(batch,),
        in_specs=[pl.BlockSpec((1, MEM_LEN, D_MODEL), lambda b: (b, 0, 0)), _resident(g.shape),
                  _resident(w.shape), pl.BlockSpec((1, seq, BRANCH_W), lambda b: (b, 0, 0))],
        out_specs=[pl.BlockSpec((1, MEM_LEN * KV_ROWS, LANES), lambda b: (b, 0, 0)),
                   pl.BlockSpec((1, seq, BRANCH_W), lambda b: (b, 0, 0))],
        out_shape=[jax.ShapeDtypeStruct((batch, MEM_LEN * KV_ROWS, LANES), F32),
                   jax.ShapeDtypeStruct((batch, seq, BRANCH_W), BF16)],
        compiler_params=_params(
            ("parallel",),
            pipelined=[((MEM_LEN, D_MODEL), F32), ((seq, BRANCH_W), BF16), ((MEM_LEN, kvw), F32),
                       ((seq, BRANCH_W), BF16)],
            resident=[(g.shape, g.dtype), (w.shape, w.dtype)]),
        name="mem_prompt",
    )(mem, g, w, mq)


def _sample_attn_body(q_ref, xn_ref, c0_ref, c1_ref, c2_ref, cm_ref, b0_ref, b1_ref, b2_ref, bm_ref,
                      bn_ref, oswa_ref, omem_ref, *, bb):
    half = SWA_HEADS

    def kv_swapped(x):
        x3 = x.reshape(x.shape[0] // SUBLANES, SUBLANES, LANES)
        return pltpu.roll(x3, half, axis=1).reshape(x.shape)

    def nt(q, x):
        return lax.dot_general(q, x, _NT, preferred_element_type=F32)

    biases = (b0_ref[...], b1_ref[...], b2_ref[...])
    items = []
    for i in range(bb):
        caches = (c0_ref[i], c1_ref[i].reshape(-1, LANES), c2_ref[i].reshape(-1, LANES))
        for g in range(N_SWA):
            items.append((q_ref[i, g], caches[g], biases[g], xn_ref[i, g], bn_ref[g]))
        items.append((q_ref[i, N_SWA], cm_ref[i], bm_ref[...], None, None))
    qs = [it[0] for it in items]
    xks = [it[1].astype(BF16) for it in items]
    xvs = [kv_swapped(it[1]).astype(BF16) for it in items]
    nks = [None if it[3] is None else it[3].astype(BF16) for it in items]
    nvs = [None if it[3] is None else kv_swapped(it[3]).astype(BF16) for it in items]
    ss = [nt(q, xk) + it[2] for q, xk, it in zip(qs, xks, items)]
    sns = [None if nk is None else nt(q, nk) + it[4] for q, nk, it in zip(qs, nks, items)]
    ms = [jnp.max(s, axis=-1, keepdims=True) for s in ss]
    ms = [m if sn is None else jnp.maximum(m, jnp.max(sn, axis=-1, keepdims=True)) for m, sn in zip(ms, sns)]
    ps = [jnp.exp(s - m) for s, m in zip(ss, ms)]
    pns = [None if sn is None else jnp.exp(sn - m) for sn, m in zip(sns, ms)]
    ls = [jnp.sum(p, axis=-1, keepdims=True) for p in ps]
    ls = [l if pn is None else l + jnp.sum(pn, axis=-1, keepdims=True) for l, pn in zip(ls, pns)]
    os = [_dot(p.astype(BF16), xv) for p, xv in zip(ps, xvs)]
    os = [o if pn is None else o + _dot(pn.astype(BF16), nv) for o, pn, nv in zip(os, pns, nvs)]
    outs = [o / l for o, l in zip(os, ls)]
    lses = [m + jnp.log(l) for m, l in zip(ms, ls)]
    per_seq = N_SWA + 1
    for i in range(bb):
        o3, l3 = outs[i * per_seq:i * per_seq + N_SWA], lses[i * per_seq:i * per_seq + N_SWA]
        mx = jnp.maximum(jnp.maximum(l3[0], l3[1]), l3[2])
        ws = [jnp.exp(l - mx) for l in l3]
        num = ws[0] * o3[0] + ws[1] * o3[1] + ws[2] * o3[2]
        oswa_ref[i] = (num / (ws[0] + ws[1] + ws[2])).astype(BF16)
        omem_ref[i] = outs[i * per_seq + N_SWA].astype(BF16)


def _sample_bias_tables(dec_seq):
    rows_q = dec_seq * SWA_HEADS
    t = (np.arange(rows_q) // SWA_HEADS)[:, None]
    hq = (np.arange(rows_q) % SWA_HEADS)[:, None]

    def table(ok):
        return np.where(ok, 0.0, NEG).astype(np.float32)

    cache_tabs, new_tabs = [], []
    for win, dil in SWA_GROUPS:
        length = min(win, PAST_LEN)
        res = min(dil, dec_seq)
        r = np.arange((length // dil) * res * 2 * SWA_HEADS)[None, :]
        m, rr, kv, hl = r // (res * 8), (r // 8) % res, (r // 4) % 2, r % 4
        delta = length + t - (m * dil + rr)
        ok = (kv == 0) & (hl == hq) & (delta % dil == 0) & (delta // dil >= 0) & (delta // dil <= SWA_BAND)
        cache_tabs.append(table(ok))
        rn = np.arange(dec_seq * 2 * SWA_HEADS)[None, :]
        tn, kvn, hn = rn // 8, (rn // 4) % 2, rn % 4
        dn = t - tn
        okn = (kvn == 0) & (hn == hq) & (dn % dil == 0) & (dn // dil >= 0) & (dn // dil <= SWA_BAND)
        new_tabs.append(table(okn))
    rm = np.arange(MEM_LEN * 2 * MEM_HEADS)[None, :]
    mem_tab = table(((rm // 4) % 2 == 0) & (rm % 4 == hq))
    return cache_tabs, np.stack(new_tabs), mem_tab


def _sample_attn(qall, xnew, cache_swa, cache_mem, dec_batch, dec_seq):
    rows_q = dec_seq * SWA_HEADS
    cache_tabs, new_tab, mem_tab = _sample_bias_tables(dec_seq)
    kvh = 2 * SWA_HEADS
    bb = math.gcd(dec_batch, SAMPLE_ATTN_SEQS_PER_STEP)
    views, specs, blocks = [], [], []
    for (win, dil), c in zip(SWA_GROUPS, cache_swa):
        length = min(win, PAST_LEN)
        res = min(dil, dec_seq)
        if dil == 1:
            v = c.reshape(dec_batch, length * kvh, LANES)
            blk = (bb, length * kvh, LANES)
            specs.append(pl.BlockSpec(blk, lambda b: (b, 0, 0)))
        else:
            v = c.reshape(dec_batch, length // dil, dil * kvh, LANES)
            blk = (bb, length // dil, res * kvh, LANES)
            specs.append(pl.BlockSpec(blk, lambda b: (b, 0, 0, 0)))
        views.append(v)
        blocks.append((blk, F32))
    cm = cache_mem.reshape(dec_batch, MEM_LEN * kvh, LANES)
    cm_blk = (bb, MEM_LEN * kvh, LANES)
    consts = [jnp.asarray(tb) for tb in cache_tabs] + [jnp.asarray(mem_tab), jnp.asarray(new_tab)]
    out_blk = pl.BlockSpec((bb, rows_q, HEAD_DIM), lambda b: (b, 0, 0))
    return pl.pallas_call(
        functools.partial(_sample_attn_body, bb=bb),
        grid=(dec_batch // bb,),
        in_specs=[pl.BlockSpec((bb, N_SWA + 1, rows_q, HEAD_DIM), lambda b: (b, 0, 0, 0)),
                  pl.BlockSpec((bb, N_SWA, dec_seq * kvh, HEAD_DIM), lambda b: (b, 0, 0, 0))]
        + specs + [pl.BlockSpec(cm_blk, lambda b: (b, 0, 0))] + [_resident(c.shape) for c in consts],
        out_specs=[out_blk, out_blk],
        out_shape=[jax.ShapeDtypeStruct((dec_batch, rows_q, HEAD_DIM), BF16)] * 2,
        compiler_params=_params(
            ("parallel",),
            pipelined=blocks + [(cm_blk, F32)],
            resident=[(c.shape, c.dtype) for c in consts]),
        name="sample_attn",
    )(qall, xnew, *views, cm, *consts)


def _merge_ffn_body(h_ref, gate_ref, oa_ref, ob_ref, oc_ref, wbr_ref, wo_ref, g_ref, wi_ref, w2o_ref, gf_ref,
                    o_ref, mix_ref, hid_ref):
    branches = (oa_ref[...], ob_ref[...], oc_ref[...])
    for c in range(D_MODEL // MXU_DIM):
        cols = slice(c * MXU_DIM, (c + 1) * MXU_DIM)
        mix = None
        for n, o_b in enumerate(branches):
            gate = gate_ref[:, n * D_MODEL + c * MXU_DIM:n * D_MODEL + (c + 1) * MXU_DIM].astype(F32)
            term = gate * _dot(o_b, wbr_ref[n, :, cols])
            mix = term if mix is None else mix + term
        mix_ref[:, cols] = mix.astype(BF16)
    h = h_ref[...] + _dot(mix_ref[...], wo_ref[...])
    o_ref[...] = _rms(_half_step(h, g_ref, wi_ref, w2o_ref, hid_ref), gf_ref[...])


def _merge_ffn(h, gates, o_swa, o_gla, o_mem, wbr, wo, g, wi, w2o, g_final):
    n = h.shape[0]
    tm = min(TOKEN_TILE, n)
    row = lambda w: pl.BlockSpec((tm, w), lambda i: (i, 0))
    weights = [wbr, wo, g, wi, w2o, g_final]
    scratch = [((tm, D_MODEL), BF16), ((tm, D_FF), BF16)]
    return pl.pallas_call(
        _merge_ffn_body,
        grid=(n // tm,),
        in_specs=[row(D_MODEL), row(N_BRANCH * D_MODEL), row(BRANCH_W), row(BRANCH_W), row(BRANCH_W)]
        + [_resident(a.shape) for a in weights],
        out_specs=row(D_MODEL),
        out_shape=jax.ShapeDtypeStruct((n, D_MODEL), F32),
        scratch_shapes=[pltpu.VMEM(s, d) for s, d in scratch],
        compiler_params=_params(
            ("parallel",),
            pipelined=[((tm, D_MODEL), F32)] * 2 + [((tm, N_BRANCH * D_MODEL), BF16)]
            + [((tm, BRANCH_W), BF16)] * 3,
            resident=[(a.shape, a.dtype) for a in weights],
            scratch=scratch),
        name="merge_ffn",
    )(h, gates, o_swa, o_gla, o_mem, *weights)


def _rope_tables(positions):
    half = HEAD_DIM // 2
    inv = ROPE_THETA ** (-np.arange(half, dtype=np.float64) / half)
    ang = np.asarray(positions, np.float64)[:, None] * inv[None, :]
    cos, sin = np.cos(ang), np.sin(ang)
    return (jnp.asarray(np.concatenate([cos, cos], axis=1), F32),
            jnp.asarray(np.concatenate([-sin, sin], axis=1), F32))


def kernel(x_prompt, x_sample, mem_prompt, cache_swa0, cache_swa1, cache_swa2, cache_mem_kv, state_gla,
           g_ffn1, w_ffn1_in, w_ffn1_out, g_mix, w_in, w_gla_a2, b_gla_a, g_gla_out, w_gate, w_branch,
           w_out, g_mem, w_mem_kv, g_ffn2, w_ffn2_in, w_ffn2_out, g_final):
    batch, seq, _ = x_prompt.shape
    dec_batch, dec_seq, _ = x_sample.shape
    depth = g_ffn1.shape[0]
    assert depth == 1, "single-layer step"
    l = 0
    row = lambda g: g.reshape(1, -1)

    w1i, w2i = w_ffn1_in[l].astype(BF16), w_ffn2_in[l].astype(BF16)
    w1o, w2o = w_ffn1_out[l].astype(BF16), w_ffn2_out[l].astype(BF16)
    wi = w_in[l].astype(BF16)
    o_gla0 = 3 * SWA_W
    o_ga = o_gla0 + 2 * GLA_QW + 2 * GLA_VW
    w_swa = wi[:, :o_gla0]
    w_g = wi[:, o_gla0:o_ga]
    w_ga = jnp.pad(wi[:, o_ga:o_ga + GLA_RANK], ((0, 0), (0, LANES - GLA_RANK)))
    w_a2 = jnp.pad(w_gla_a2[l].astype(BF16), ((0, LANES - GLA_RANK), (0, 0)))
    w_mq = wi[:, o_ga + GLA_RANK:]
    wgate = w_gate[l].astype(BF16)
    wbr = w_branch[l].astype(BF16)
    wo = w_out[l].astype(BF16)
    wmem = w_mem_kv[l].astype(BF16)
    weights_misc = (row(g_mix[l]), w_g, w_ga, w_a2, row(b_gla_a[l]), w_mq, wgate)

    n_p = batch * seq
    cos_p, sin_p = _rope_tables(np.arange(seq))
    hp = _ffn(x_prompt.reshape(n_p, D_MODEL), row(g_ffn1[l]), w1i, w1o)
    *slabs, rows0, rows1, rows2 = _proj_swa_prompt(hp, row(g_mix[l]), cos_p, sin_p, w_swa, batch, seq)
    gq, gk, gv, gr, la, mq, gates = _proj_misc(hp, *weights_misc)
    o_swa = _swa_prompt(slabs, batch, seq)
    r3 = lambda a: a.reshape(batch, seq, a.shape[-1])
    o_gla, gla_p = _gla(r3(gq), r3(gk), r3(la), r3(gv), r3(gr), row(g_gla_out[l]), None, batch, seq, 1)
    kv_p, o_mem = _mem_prompt(mem_prompt, row(g_mem[l]), wmem, r3(mq), batch, seq)
    tail_weights = (wbr, wo, row(g_ffn2[l]), w2i, w2o, row(g_final))
    y_prompt = _merge_ffn(hp, gates, o_swa.reshape(n_p, BRANCH_W), o_gla.reshape(n_p, BRANCH_W),
                          o_mem.reshape(n_p, BRANCH_W), *tail_weights).reshape(batch, seq, D_MODEL)

    swa_p = [r.reshape(1, batch, min(win, seq), 2, SWA_HEADS, HEAD_DIM)
             for r, (win, _) in zip((rows0, rows1, rows2), SWA_GROUPS)]
    mem_kv_p = kv_p.reshape(1, batch, MEM_LEN, 2, MEM_HEADS, HEAD_DIM)

    n_s = dec_batch * dec_seq
    cos_s, sin_s = _rope_tables(PAST_LEN + (np.arange(n_s) % dec_seq))
    hs = _ffn(x_sample.reshape(n_s, D_MODEL), row(g_ffn1[l]), w1i, w1o)
    slabs_s = _proj_swa_sample(hs, row(g_mix[l]), cos_s, sin_s, w_swa)
    gq, gk, gv, gr, la, mq, gates = _proj_misc(hs, *weights_misc)
    s3 = lambda a: a.reshape(dec_batch, dec_seq, a.shape[-1])
    o_gla_s, gla_s = _gla(s3(gq), s3(gk), s3(la), s3(gv), s3(gr), row(g_gla_out[l]), state_gla[l],
                          dec_batch, dec_seq, SAMPLE_SEQS_PER_STEP)
    s6 = slabs_s.reshape(3, N_SWA, SWA_HEADS, dec_batch, dec_seq, HEAD_DIM)
    q_swa = jnp.transpose(s6[0], (2, 0, 3, 1, 4)).reshape(dec_batch, N_SWA, dec_seq * SWA_HEADS, HEAD_DIM)
    q_mem = mq.reshape(dec_batch, 1, dec_seq * MEM_HEADS, HEAD_DIM)
    qall = jnp.concatenate([q_swa.astype(BF16), q_mem], axis=1)
    xnew = jnp.transpose(s6[1:], (3, 1, 4, 0, 2, 5))
    o_swa_s, o_mem_s = _sample_attn(
        qall, xnew.reshape(dec_batch, N_SWA, dec_seq * 2 * SWA_HEADS, HEAD_DIM),
        (cache_swa0[l], cache_swa1[l], cache_swa2[l]), cache_mem_kv[l], dec_batch, dec_seq)
    y_sample = _merge_ffn(hs, gates, o_swa_s.reshape(n_s, BRANCH_W), o_gla_s.reshape(n_s, BRANCH_W),
                          o_mem_s.reshape(n_s, BRANCH_W), *tail_weights).reshape(dec_batch, dec_seq, D_MODEL)
    swa_s = [xnew[:, g][None] for g in range(N_SWA)]

    return (y_prompt, y_sample, swa_p[0], swa_p[1], swa_p[2], mem_kv_p, gla_p[None],
            swa_s[0], swa_s[1], swa_s[2], gla_s[None])
```
